```python
import jax, jax.numpy as jnp
from jax import lax
import numpy as np

D_MODEL = 2048
BATCH = 8
SEQ = 2048
DEPTH = 2
DEC_BATCH = 32
DEC_SEQ = 8
PAST_LEN = 8192
PAGE_SIZE = 128

HEAD_DIM = 64
CONV_HEADS = D_MODEL // 128
SB_HEADS = D_MODEL // 128
CONV_CH = CONV_HEADS * HEAD_DIM
SB_WIDTH = SB_HEADS * HEAD_DIM
MIX_WIDTH = CONV_CH + SB_WIDTH
IN_WIDTH = 3 * CONV_CH + 3 * SB_WIDTH
CONV_K = 3
FFN_CONV_K = 3
D_FF = 11 * D_MODEL // 4
PLE_DIM = 256
Q_BLOCK = 128
LN_EPS = 1e-5
SB_SCALE = HEAD_DIM ** -0.5
SB_BIAS_INIT = -7.0
ALPHA = (2.0 * DEPTH) ** 0.25
BETA = (8.0 * DEPTH) ** -0.25

kernel_name = 'hybrid_conv_stickbreak_deepnorm_step'


def layer_norm(x, g, b):
    xf = x.astype(jnp.float32)
    mu = jnp.mean(xf, axis=-1, keepdims=True)
    var = jnp.mean(jnp.square(xf - mu), axis=-1, keepdims=True)
    y = (xf - mu) * lax.rsqrt(var + LN_EPS) * g.astype(jnp.float32) + b.astype(jnp.float32)
    return y.astype(x.dtype)


def causal_dwconv(u, buf, w):
    k_w = w.shape[0]
    t = u.shape[1]
    up = jnp.concatenate([buf.astype(u.dtype), u], axis=1)
    y = up[:, 0:t] * w[0]
    for j in range(1, k_w):
        y = y + up[:, j:j + t] * w[j]
    return y, up[:, -(k_w - 1):]


def sb_block(qb, pos_b, k, v, k_pos, sb_bias):
    z = (jnp.einsum('bqhd,bkhd->bhqk', qb, k).astype(jnp.float32) * SB_SCALE
         + sb_bias.astype(jnp.float32)[None, :, None, None])
    valid = k_pos[None, :] < pos_b[:, None]
    lsn = jnp.where(valid, jax.nn.log_sigmoid(-z), 0.0)
    log_a = jax.nn.log_sigmoid(z) + lax.cumsum(lsn, axis=3, reverse=True) - lsn
    a = jnp.where(valid, jnp.exp(log_a), 0.0)
    return jnp.einsum('bhqk,bkhd->bqhd', a.astype(v.dtype), v)


def sb_attention(q, k, v, past_len, sb_bias):
    bsz, t, h, dh = q.shape
    blk = min(Q_BLOCK, t)
    n_blk = -(-t // blk)
    pad = n_blk * blk - t
    qp = jnp.pad(q, ((0, 0), (0, pad), (0, 0), (0, 0)))
    qb = qp.reshape(bsz, n_blk, blk, h, dh).transpose(1, 0, 2, 3, 4)
    pb = (past_len + jnp.arange(n_blk * blk, dtype=jnp.int32)).reshape(n_blk, blk)
    k_pos = jnp.arange(k.shape[1], dtype=jnp.int32)
    out = lax.map(lambda args: sb_block(args[0], args[1], k, v, k_pos, sb_bias), (qb, pb))
    out = out.transpose(1, 0, 2, 3, 4).reshape(bsz, n_blk * blk, h, dh)
    return out[:, :t]


def layer_forward(x, p_i, k_past, v_past, mix_buf, ffn_buf,
                  w_in, w_conv_mix, sb_bias, w_out, ln1_g, ln1_b, w_up, w_ffn_conv,
                  b_ffn_conv, w_down, w_pg, w_pe, ln2_g, ln2_b):
    bsz, t, _ = x.shape
    past_len = k_past.shape[1]
    h = x @ w_in
    cuts = [CONV_CH, 2 * CONV_CH, 3 * CONV_CH, 3 * CONV_CH + SB_WIDTH, 3 * CONV_CH + 2 * SB_WIDTH]
    cb, cc, cx, q, k, v = jnp.split(h, cuts, axis=-1)
    conv_y, new_mix_buf = causal_dwconv(cc * cx, mix_buf, w_conv_mix)
    z_conv = cb * conv_y
    q = q.reshape(bsz, t, SB_HEADS, HEAD_DIM)
    k = k.reshape(bsz, t, SB_HEADS, HEAD_DIM)
    v = v.reshape(bsz, t, SB_HEADS, HEAD_DIM)
    k_all = jnp.concatenate([k_past.astype(k.dtype), k], axis=1)
    v_all = jnp.concatenate([v_past.astype(v.dtype), v], axis=1)
    o = sb_attention(q, k_all, v_all, past_len, sb_bias).reshape(bsz, t, SB_WIDTH)
    mix = jnp.concatenate([z_conv, o], axis=-1) @ w_out
    x = layer_norm(ALPHA * x + mix, ln1_g, ln1_b)
    a, g = jnp.split(x @ w_up, 2, axis=-1)
    a_c, new_ffn_buf = causal_dwconv(a, ffn_buf, w_ffn_conv)
    ffn = (jax.nn.gelu(a_c + b_ffn_conv) * g) @ w_down
    ple = jax.nn.sigmoid(x @ w_pg) * (p_i @ w_pe)
    x = layer_norm(ALPHA * x + ffn + ple, ln2_g, ln2_b)
    return x, k, v, new_mix_buf, new_ffn_buf


def setup_inputs(seed: int = 0) -> dict:
    key = jax.random.key(seed)
    ks = jax.random.split(key, 24)
    n_pages = PAST_LEN // PAGE_SIZE
    n_used = DEC_BATCH * n_pages
    n_pool = n_used + (n_used + 3) // 4

    def nrm(k, shape, s):
        return jax.random.normal(k, shape, jnp.float32) * s

    x_prompt = nrm(ks[0], (BATCH, SEQ, D_MODEL), 1.0)
    x_sample = nrm(ks[1], (DEC_BATCH, DEC_SEQ, D_MODEL), 1.0)
    cache_k = nrm(ks[2], (DEPTH, n_pool, PAGE_SIZE, SB_HEADS, HEAD_DIM), 1.0)
    cache_v = nrm(ks[3], (DEPTH, n_pool, PAGE_SIZE, SB_HEADS, HEAD_DIM), 1.0)
    state_conv_mix = nrm(ks[4], (DEPTH, DEC_BATCH, CONV_K - 1, CONV_CH), 1.0)
    state_conv_ffn = nrm(ks[5], (DEPTH, DEC_BATCH, FFN_CONV_K - 1, D_FF), 1.0)
    page_table = jax.random.permutation(ks[6], n_pool)[:n_used].reshape(DEC_BATCH, n_pages).astype(jnp.int32)
    p_prompt = nrm(ks[7], (DEPTH, BATCH, SEQ, PLE_DIM), 1.0)
    p_sample = nrm(ks[8], (DEPTH, DEC_BATCH, DEC_SEQ, PLE_DIM), 1.0)
    w_in = nrm(ks[9], (DEPTH, D_MODEL, IN_WIDTH), D_MODEL ** -0.5)
    w_conv_mix = nrm(ks[10], (DEPTH, CONV_K, CONV_CH), CONV_K ** -0.5)
    sb_bias = SB_BIAS_INIT + nrm(ks[22], (DEPTH, SB_HEADS), 0.1)
    w_out = nrm(ks[11], (DEPTH, MIX_WIDTH, D_MODEL), MIX_WIDTH ** -0.5 * BETA)
    ln1_g = 1.0 + nrm(ks[12], (DEPTH, D_MODEL), 0.01)
    ln1_b = nrm(ks[13], (DEPTH, D_MODEL), 0.01)
    w_up = nrm(ks[14], (DEPTH, D_MODEL, 2 * D_FF), D_MODEL ** -0.5)
    w_ffn_conv = nrm(ks[15], (DEPTH, FFN_CONV_K, D_FF), FFN_CONV_K ** -0.5)
    b_ffn_conv = nrm(ks[16], (DEPTH, D_FF), 0.01)
    w_down = nrm(ks[17], (DEPTH, D_FF, D_MODEL), D_FF ** -0.5 * BETA)
    w_pg = nrm(ks[18], (DEPTH, D_MODEL, D_MODEL), D_MODEL ** -0.5)
    w_pe = nrm(ks[19], (DEPTH, PLE_DIM, D_MODEL), PLE_DIM ** -0.5 * BETA)
    ln2_g = 1.0 + nrm(ks[20], (DEPTH, D_MODEL), 0.01)
    ln2_b = nrm(ks[21], (DEPTH, D_MODEL), 0.01)
    return {'x_prompt': x_prompt, 'x_sample': x_sample,
            'cache_k': cache_k, 'cache_v': cache_v,
            'state_conv_mix': state_conv_mix, 'state_conv_ffn': state_conv_ffn,
            'page_table': page_table, 'p_prompt': p_prompt, 'p_sample': p_sample,
            'w_in': w_in, 'w_conv_mix': w_conv_mix, 'sb_bias': sb_bias, 'w_out': w_out,
            'ln1_g': ln1_g, 'ln1_b': ln1_b, 'w_up': w_up, 'w_ffn_conv': w_ffn_conv,
            'b_ffn_conv': b_ffn_conv, 'w_down': w_down, 'w_pg': w_pg, 'w_pe': w_pe,
            'ln2_g': ln2_g, 'ln2_b': ln2_b}


def reference(x_prompt, x_sample, cache_k, cache_v, state_conv_mix, state_conv_ffn,
              page_table, p_prompt, p_sample, w_in, w_conv_mix, sb_bias, w_out, ln1_g, ln1_b,
              w_up, w_ffn_conv, b_ffn_conv, w_down, w_pg, w_pe, ln2_g, ln2_b):
    bsz = x_prompt.shape[0]
    dbsz = x_sample.shape[0]
    n_pages = page_table.shape[1]
    past_len = n_pages * PAGE_SIZE
    xp, xs = x_prompt, x_sample
    kp_l, vp_l, cmp_l, cfp_l = [], [], [], []
    ks_l, vs_l, cms_l, cfs_l = [], [], [], []
    for i in range(DEPTH):
        w = (w_in[i], w_conv_mix[i], sb_bias[i], w_out[i], ln1_g[i], ln1_b[i], w_up[i],
             w_ffn_conv[i], b_ffn_conv[i], w_down[i], w_pg[i], w_pe[i], ln2_g[i], ln2_b[i])
        empty_kv = jnp.zeros((bsz, 0, SB_HEADS, HEAD_DIM), xp.dtype)
        zmix = jnp.zeros((bsz, CONV_K - 1, CONV_CH), xp.dtype)
        zffn = jnp.zeros((bsz, FFN_CONV_K - 1, D_FF), xp.dtype)
        xp, kp, vp, cmp_, cfp = layer_forward(xp, p_prompt[i], empty_kv, empty_kv, zmix, zffn, *w)
        k_past = cache_k[i][page_table].reshape(dbsz, past_len, SB_HEADS, HEAD_DIM)
        v_past = cache_v[i][page_table].reshape(dbsz, past_len, SB_HEADS, HEAD_DIM)
        xs, ksm, vsm, cms, cfs = layer_forward(xs, p_sample[i], k_past, v_past,
                                               state_conv_mix[i], state_conv_ffn[i], *w)
        kp_l.append(kp); vp_l.append(vp); cmp_l.append(cmp_); cfp_l.append(cfp)
        ks_l.append(ksm); vs_l.append(vsm); cms_l.append(cms); cfs_l.append(cfs)
    return (xp, xs,
            jnp.stack(kp_l), jnp.stack(vp_l), jnp.stack(cmp_l), jnp.stack(cfp_l),
            jnp.stack(ks_l), jnp.stack(vs_l), jnp.stack(cms_l), jnp.stack(cfs_l))
```

```python
import functools

import jax
import jax.numpy as jnp
from jax import lax
from jax.experimental import pallas as pl
from jax.experimental.pallas import tpu as pltpu

BF16 = jnp.bfloat16
F32 = jnp.float32

LN_EPS = 1e-5
LANES = 128
SUBLANES = 8
VMEM_LIMIT = 56 * 1024 * 1024

_dot = functools.partial(jnp.dot, preferred_element_type=F32)


def _dot_nt(a, b):
    return lax.dot_general(a, b, (((1,), (1,)), ((), ())), preferred_element_type=F32)


def _params(n_grid):
    return pltpu.CompilerParams(dimension_semantics=("arbitrary",) * n_grid,
                                vmem_limit_bytes=VMEM_LIMIT)


def _shift_rows_carry(u, prev8, shift):
    p = pltpu.roll(u, shift, axis=0)
    row8 = lax.broadcasted_iota(jnp.int32, prev8.shape, 0)
    top = jnp.where(row8 < shift, pltpu.roll(prev8, shift, axis=0), p[0:SUBLANES])
    return jnp.concatenate([top, p[SUBLANES:]], axis=0)


def _conv3_carry(u, prev8, w):
    return (w[0:1] * _shift_rows_carry(u, prev8, 2) + w[1:2] * _shift_rows_carry(u, prev8, 1)
            + w[2:3] * u)


def _conv3_state(u, e, w, seq_len):
    tm = u.shape[0]
    pos = lax.broadcasted_iota(jnp.int32, u.shape, 0) % seq_len
    p1 = jnp.where(pos >= 1, pltpu.roll(u, 1, axis=0), pltpu.roll(e, tm - 1, axis=0))
    p2 = jnp.where(pos >= 2, pltpu.roll(u, 2, axis=0), e)
    return w[0:1] * p2 + w[1:2] * p1 + w[2:3] * u


def _layer_norm(y, g, b):
    mu = jnp.mean(y, axis=-1, keepdims=True)
    d = y - mu
    var = jnp.mean(d * d, axis=-1, keepdims=True)
    return d * lax.rsqrt(var + LN_EPS) * g + b


def _gelu_tanh(x):
    return x * (0.5 * (1.0 + jnp.tanh(0.7978845608028654 * (x + 0.044715 * (x * x * x)))))


def _log1m_beta(z):
    return -(jnp.maximum(z, 0.0) + jnp.log(1.0 + jnp.exp(-jnp.abs(z))))


def _suffix_sums(lsn, tri2):
    hi = lsn.astype(BF16)
    lo = (lsn - hi.astype(F32)).astype(BF16)
    return _dot(jnp.concatenate([hi, lo], axis=1), tri2)


def _tri2(n):
    r = lax.broadcasted_iota(jnp.int32, (2 * n, n), 0)
    c = lax.broadcasted_iota(jnp.int32, (2 * n, n), 1)
    r = jnp.where(r >= n, r - n, r)
    return jnp.where(r >= c, 1.0, 0.0).astype(BF16)


def _in_proj_kernel(*refs, seq_len, carry_mode):
    if carry_mode:
        (x_ref, wb_ref, wc_ref, wx_ref, wq_ref, wk_ref, wv_ref, wconv_ref,
         z_ref, q_ref, kt_ref, vt_ref, vb_ref, tail_ref, carry_ref) = refs
    else:
        (x_ref, wb_ref, wc_ref, wx_ref, wq_ref, wk_ref, wv_ref, wconv_ref, e_ref,
         z_ref, q_ref, k_ref, v_ref, tail_ref) = refs
    tm = x_ref.shape[0]
    xb = x_ref[...].astype(BF16)
    u = _dot(xb, wc_ref[...]) * _dot(xb, wx_ref[...])
    if carry_mode:
        i = pl.program_id(1)

        @pl.when(i % (seq_len // tm) == 0)
        def _():
            carry_ref[...] = jnp.zeros_like(carry_ref)

        y = _conv3_carry(u, carry_ref[...], wconv_ref[...])
        carry_ref[...] = u[tm - SUBLANES:, :]
        tail_ref[...] = u[tm - SUBLANES:, :]
    else:
        y = _conv3_state(u, e_ref[...], wconv_ref[...], seq_len)
        tail_ref[...] = u
    z_ref[...] = (_dot(xb, wb_ref[...]) * y).astype(z_ref.dtype)
    q_ref[...] = _dot(xb, wq_ref[...]).astype(q_ref.dtype)
    k = _dot(xb, wk_ref[...])
    v = _dot(xb, wv_ref[...])
    if carry_mode:
        kt_ref[...] = k.T
        vt_ref[...] = v.T
        vb_ref[...] = v.astype(BF16)
    else:
        k_ref[...] = k
        v_ref[...] = v


def _in_proj(x, w_in_b, w_conv, layer, *, seq_len, q_dtype, state=None):
    m, d = x.shape
    c = w_conv.shape[-1]
    carry_mode = state is None
    tm = min(512, m)
    tn = 512
    nb = c // tn
    batch = m // seq_len
    grid = (nb, m // tm)

    def wspec(s):
        return pl.BlockSpec((None, d, tn), lambda j, i, s=s: (layer, 0, s * nb + j))

    in_specs = [pl.BlockSpec((tm, d), lambda j, i: (i, 0))]
    in_specs += [wspec(s) for s in range(6)]
    in_specs += [pl.BlockSpec((None, w_conv.shape[1], tn), lambda j, i: (layer, 0, j))]
    args = [x] + [w_in_b] * 6 + [w_conv]
    tile = pl.BlockSpec((tm, tn), lambda j, i: (i, j))
    rows = lambda dt: jax.ShapeDtypeStruct((m, c), dt)
    scratch = []
    if carry_mode:
        assert seq_len % tm == 0
        per_seq = seq_len // tm
        tspec = pl.BlockSpec((None, tn, tm), lambda j, i: (i // per_seq, j, i % per_seq))
        tshape = jax.ShapeDtypeStruct((batch, c, seq_len), F32)
        tail_shape = jax.ShapeDtypeStruct((m // tm, SUBLANES, c), F32)
        tail_spec = pl.BlockSpec((None, SUBLANES, tn), lambda j, i: (i, 0, j))
        scratch = [pltpu.VMEM((SUBLANES, tn), F32)]
        out_shape = (rows(BF16), rows(q_dtype), tshape, tshape, rows(BF16), tail_shape)
        out_specs = (tile, tile, tspec, tspec, tile, tail_spec)
    else:
        assert tm == m and seq_len == SUBLANES
        in_specs.append(tile)
        args.append(state)
        out_shape = (rows(BF16), rows(q_dtype), rows(F32), rows(F32), rows(F32))
        out_specs = (tile,) * 5
    return pl.pallas_call(
        functools.partial(_in_proj_kernel, seq_len=seq_len, carry_mode=carry_mode),
        grid=grid, in_specs=in_specs, out_specs=out_specs,
        out_shape=out_shape, scratch_shapes=scratch, compiler_params=_params(2),
        name="in_proj")(*args)


def _attn_prompt_kernel(bias_ref, q_ref, kt_ref, v_ref, o_ref,
                        kb_ref, tri_ref, acc_ref, car_ref, *, layer, tk, head_dim, scale):
    c = pl.program_id(1)
    qi = pl.program_id(2)
    tq = q_ref.shape[0]
    heads_per_tile = LANES // head_dim
    nk = kt_ref.shape[1] // tk

    @pl.when(qi == 0)
    def _():
        chan = lax.broadcasted_iota(jnp.int32, (LANES, tk), 0)
        for j in range(nk):
            kt = kt_ref[:, j * tk:(j + 1) * tk]
            for h in range(heads_per_tile):
                sel = (chan >= h * head_dim) & (chan < (h + 1) * head_dim)
                kb_ref[j, h] = jnp.where(sel, kt, 0.0).astype(BF16)
        tri_ref[...] = _tri2(tk)

    q = q_ref[...]
    acc_ref[...] = jnp.zeros_like(acc_ref)
    car_ref[...] = jnp.zeros_like(car_ref)

    def block(j, masked):
        vblk = v_ref[pl.ds(pl.multiple_of(j * tk, tk), tk), :]
        if masked:
            row = lax.broadcasted_iota(jnp.int32, (tq, tk), 0)
            col = lax.broadcasted_iota(jnp.int32, (tq, tk), 1)
            valid = col < row
        for h in range(heads_per_tile):
            bias = bias_ref[layer, c * heads_per_tile + h]
            z = _dot(q, kb_ref[j, h]) * scale + bias
            lsn = _log1m_beta(z)
            if masked:
                lsn = jnp.where(valid, lsn, 0.0)
            local = _suffix_sums(lsn, tri_ref[...])
            carry = car_ref[h]
            a = jnp.exp(z + local + jnp.tile(carry, (1, tk // LANES)))
            if masked:
                a = jnp.where(valid, a, 0.0)
            acc_ref[h] += _dot(a.astype(BF16), vblk)
            car_ref[h] = carry + jnp.broadcast_to(local[:, 0:1], carry.shape)

    block(qi, True)

    def body(it, carry):
        block(qi - 1 - it, False)
        return carry

    lax.fori_loop(0, qi, body, 0)

    lane = lax.broadcasted_iota(jnp.int32, (tq, LANES), 1)
    out = acc_ref[0]
    for h in range(1, heads_per_tile):
        out = jnp.where(lane >= h * head_dim, acc_ref[h], out)
    o_ref[...] = out.astype(o_ref.dtype)


def _attn_prompt(q, kt, v, sb_bias, layer, *, head_dim):
    m, w = q.shape
    batch, _, seq_len = kt.shape
    tq = tk = 256
    nq = seq_len // tq
    heads_per_tile = LANES // head_dim
    grid = (batch, w // LANES, nq)
    qspec = pl.BlockSpec((tq, LANES), lambda b, c, i: (b * nq + i, c))
    return pl.pallas_call(
        functools.partial(_attn_prompt_kernel, layer=layer, tk=tk, head_dim=head_dim,
                          scale=head_dim ** -0.5),
        grid=grid,
        in_specs=[pl.BlockSpec(memory_space=pltpu.SMEM), qspec,
                  pl.BlockSpec((None, LANES, seq_len), lambda b, c, i: (b, c, 0)),
                  pl.BlockSpec((seq_len, LANES), lambda b, c, i: (b, c))],
        out_specs=qspec,
        out_shape=jax.ShapeDtypeStruct((m, w), BF16),
        scratch_shapes=[pltpu.VMEM((seq_len // tk, heads_per_tile, LANES, tk), BF16),
                        pltpu.VMEM((2 * tk, tk), BF16),
                        pltpu.VMEM((heads_per_tile, tq, LANES), F32),
                        pltpu.VMEM((heads_per_tile, tq, LANES), F32)],
        compiler_params=_params(3), name="attn_prompt")(sb_bias, q, kt, v)


def _attn_sample_kernel(pt_ref, bias_ref, q_ref, kn_ref, vn_ref, ckt_ref, cvt_ref, o_ref,
                        wt_ref, tri_ref, acc_ref, car_ref, *, heads, head_dim, scale):
    del pt_ref
    j = pl.program_id(1)
    t, w = q_ref.shape
    page = ckt_ref.shape[1]
    rows = heads * t

    def process(k, v, new_rows):
        kb, vb = k.astype(BF16), v.astype(BF16)
        s = _dot_nt(wt_ref[...], kb) if new_rows else _dot(wt_ref[...], kb)
        z = s * scale + bias_ref[...]
        lsn = _log1m_beta(z)
        if new_rows:
            row = lax.broadcasted_iota(jnp.int32, (rows, page), 0)
            col = lax.broadcasted_iota(jnp.int32, (rows, page), 1)
            valid = col < row % t
            lsn = jnp.where(valid, lsn, 0.0)
        local = _suffix_sums(lsn, tri_ref[...])
        carry = car_ref[...]
        a = jnp.exp(z + local + carry)
        if new_rows:
            a = jnp.where(valid, a, 0.0)
        ab = a.astype(BF16)
        acc_ref[...] += _dot(ab, vb) if new_rows else _dot_nt(ab, vb)
        car_ref[...] = carry + jnp.broadcast_to(local[:, 0:1], carry.shape)

    @pl.when(j == 0)
    def _():
        qt = jnp.tile(q_ref[...], (heads, 1))
        rowh = lax.broadcasted_iota(jnp.int32, (rows, w), 0) // t
        colh = lax.broadcasted_iota(jnp.int32, (rows, w), 1) // head_dim
        wt_ref[...] = jnp.where(rowh == colh, qt, 0.0).astype(BF16)
        tri_ref[...] = _tri2(page)
        acc_ref[...] = jnp.zeros_like(acc_ref)
        car_ref[...] = jnp.zeros_like(car_ref)
        pad = jnp.zeros((page - t, w), F32)
        process(jnp.concatenate([kn_ref[...], pad], axis=0),
                jnp.concatenate([vn_ref[...], pad], axis=0), True)

    @pl.when(j > 0)
    def _():
        process(ckt_ref[...], cvt_ref[...], False)

    @pl.when(j == pl.num_programs(1) - 1)
    def _():
        heads_per_tile = LANES // head_dim
        lane = lax.broadcasted_iota(jnp.int32, (t, LANES), 1)
        for p in range(w // LANES):
            cols = slice(p * LANES, (p + 1) * LANES)
            r0 = p * heads_per_tile * t
            out = acc_ref[r0:r0 + t, cols]
            for h in range(1, heads_per_tile):
                out = jnp.where(lane >= h * head_dim, acc_ref[r0 + h * t:r0 + (h + 1) * t, cols], out)
            o_ref[:, cols] = out


def _attn_sample(q, k_new, v_new, cache_kt, cache_vt, page_table, bias_rows, layer, *, seq_len, head_dim):
    m, w = q.shape
    dbsz, n_pages = page_table.shape
    page = cache_kt.shape[3]
    heads = w // head_dim
    rows = heads * seq_len
    assert page == LANES and seq_len == SUBLANES and rows % SUBLANES == 0
    grid = (dbsz, n_pages + 1)
    rowspec = pl.BlockSpec((seq_len, w), lambda b, j, pt: (b, 0))
    pagespec = pl.BlockSpec((None, None, w, page),
                            lambda b, j, pt: (layer, pt[b, n_pages - jnp.maximum(j, 1)], 0, 0))
    grid_spec = pltpu.PrefetchScalarGridSpec(
        num_scalar_prefetch=1, grid=grid,
        in_specs=[pl.BlockSpec((rows, LANES), lambda b, j, pt: (0, 0)),
                  rowspec, rowspec, rowspec, pagespec, pagespec],
        out_specs=rowspec,
        scratch_shapes=[pltpu.VMEM((rows, w), BF16),
                        pltpu.VMEM((2 * page, page), BF16),
                        pltpu.VMEM((rows, w), F32),
                        pltpu.VMEM((rows, LANES), F32)])
    return pl.pallas_call(
        functools.partial(_attn_sample_kernel, heads=heads, head_dim=head_dim, scale=head_dim ** -0.5),
        grid_spec=grid_spec, out_shape=jax.ShapeDtypeStruct((m, w), F32),
        compiler_params=_params(2), name="attn_sample")(
            page_table, bias_rows, q, k_new, v_new, cache_kt, cache_vt)


def _out_proj_kernel(z_ref, o_ref, x_ref, p_ref, wout_ref, wpg_ref, wpe_ref, g_ref, b_ref,
                     x1b_ref, r_ref, *, alpha):
    mixed = jnp.concatenate([z_ref[...].astype(BF16), o_ref[...].astype(BF16)], axis=1)
    y = alpha * x_ref[...] + _dot(mixed, wout_ref[...])
    x1 = _layer_norm(y, g_ref[...], b_ref[...])
    x1b = x1.astype(BF16)
    x1b_ref[...] = x1b
    gate = 1.0 / (1.0 + jnp.exp(-_dot(x1b, wpg_ref[...])))
    r_ref[...] = alpha * x1 + gate * _dot(p_ref[...].astype(BF16), wpe_ref[...])


def _out_proj(z, o, x, p, w_out_b, w_pg_b, w_pe_b, ln_g, ln_b, layer, *, alpha):
    m, d = x.shape
    c = z.shape[1]
    pdim = p.shape[-1]
    tm = min(256, m)
    rowspec = lambda width: pl.BlockSpec((tm, width), lambda i: (i, 0))
    const = lambda r, cc: pl.BlockSpec((None, r, cc), lambda i: (layer, 0, 0),
                                       pipeline_mode=pl.Buffered(1))
    return pl.pallas_call(
        functools.partial(_out_proj_kernel, alpha=alpha),
        grid=(m // tm,),
        in_specs=[rowspec(c), rowspec(c), rowspec(d),
                  pl.BlockSpec((None, tm, pdim), lambda i: (layer, i, 0)),
                  const(2 * c, d), const(d, d), const(pdim, d), const(1, d), const(1, d)],
        out_specs=(rowspec(d), rowspec(d)),
        out_shape=(jax.ShapeDtypeStruct((m, d), BF16), jax.ShapeDtypeStruct((m, d), F32)),
        compiler_params=_params(1), name="out_proj")(
            z, o, x, p, w_out_b, w_pg_b, w_pe_b, ln_g, ln_b)


def _ffn_kernel(*refs, seq_len, carry_mode):
    if carry_mode:
        (x1b_ref, r_ref, wa_ref, wg_ref, wconv_ref, bconv_ref, wd_ref, g_ref, b_ref,
         out_ref, tail_ref, carry_ref) = refs
    else:
        (x1b_ref, r_ref, wa_ref, wg_ref, wconv_ref, bconv_ref, wd_ref, g_ref, b_ref, e_ref,
         out_ref, tail_ref) = refs
    i = pl.program_id(0)
    f = pl.program_id(1)
    tm = x1b_ref.shape[0]
    x1b = x1b_ref[...]
    a = _dot(x1b, wa_ref[...])
    gate = _dot(x1b, wg_ref[...])
    if carry_mode:
        @pl.when(i % (seq_len // tm) == 0)
        def _():
            carry_ref[f] = jnp.zeros(carry_ref.shape[1:], F32)

        a_c = _conv3_carry(a, carry_ref[f], wconv_ref[...])
        carry_ref[f] = a[tm - SUBLANES:, :]
        tail_ref[...] = a[tm - SUBLANES:, :]
    else:
        a_c = _conv3_state(a, e_ref[...], wconv_ref[...], seq_len)
        tail_ref[...] = a
    h = (_gelu_tanh(a_c + bconv_ref[...]) * gate).astype(BF16)
    contrib = _dot(h, wd_ref[...])

    @pl.when(f == 0)
    def _():
        out_ref[...] = r_ref[...] + contrib

    @pl.when(f > 0)
    def _():
        out_ref[...] += contrib

    @pl.when(f == pl.num_programs(1) - 1)
    def _():
        out_ref[...] = _layer_norm(out_ref[...], g_ref[...], b_ref[...])


def _ffn(x1b, r, w_up_b, w_conv, b_conv, w_down_b, ln_g, ln_b, layer, *, seq_len, state=None):
    m, d = r.shape
    dff = w_conv.shape[-1]
    carry_mode = state is None
    tm = min(512, m)
    tf = 512
    nf = dff // tf
    assert dff % tf == 0
    rowspec = pl.BlockSpec((tm, d), lambda i, f: (i, 0))
    vec = lambda rows, width: pl.BlockSpec((None, rows, width), lambda i, f: (layer, 0, 0))
    in_specs = [rowspec, rowspec,
                pl.BlockSpec((None, d, tf), lambda i, f: (layer, 0, f)),
                pl.BlockSpec((None, d, tf), lambda i, f: (layer, 0, nf + f)),
                pl.BlockSpec((None, w_conv.shape[1], tf), lambda i, f: (layer, 0, f)),
                pl.BlockSpec((None, 1, tf), lambda i, f: (layer, 0, f)),
                pl.BlockSpec((None, tf, d), lambda i, f: (layer, f, 0)),
                vec(1, d), vec(1, d)]
    args = [x1b, r, w_up_b, w_up_b, w_conv, b_conv, w_down_b, ln_g, ln_b]
    scratch = []
    if carry_mode:
        assert seq_len % tm == 0
        tail_shape = jax.ShapeDtypeStruct((m // tm, SUBLANES, dff), F32)
        tail_spec = pl.BlockSpec((None, SUBLANES, tf), lambda i, f: (i, 0, f))
        scratch = [pltpu.VMEM((nf, SUBLANES, tf), F32)]
    else:
        assert tm == m and seq_len == SUBLANES
        tile = pl.BlockSpec((tm, tf), lambda i, f: (i, f))
        in_specs.append(tile)
        args.append(state)
        tail_shape = jax.ShapeDtypeStruct((m, dff), F32)
        tail_spec = tile
    return pl.pallas_call(
        functools.partial(_ffn_kernel, seq_len=seq_len, carry_mode=carry_mode),
        grid=(m // tm, nf), in_specs=in_specs, out_specs=(rowspec, tail_spec),
        out_shape=(jax.ShapeDtypeStruct((m, d), F32), tail_shape),
        scratch_shapes=scratch, compiler_params=_params(2), name="ffn")(*args)


def _state_rows(state, seq_len):
    b, k1, c = state.shape
    return jnp.pad(state, ((0, 0), (0, seq_len - k1), (0, 0))).reshape(b * seq_len, c)


def _last_rows(tails, batch, n):
    per_seq = tails.shape[0] // batch
    return tails[per_seq - 1::per_seq, SUBLANES - n:]


def kernel(x_prompt, x_sample, cache_k, cache_v, state_conv_mix, state_conv_ffn, page_table, p_prompt,
           p_sample, w_in, w_conv_mix, sb_bias, w_out, ln1_g, ln1_b, w_up, w_ffn_conv, b_ffn_conv,
           w_down, w_pg, w_pe, ln2_g, ln2_b):
    bsz, seq, d = x_prompt.shape
    dbsz, dseq, _ = x_sample.shape
    depth, n_pool, page, heads, head_dim = cache_k.shape
    width = heads * head_dim
    conv_k = w_conv_mix.shape[1]
    assert conv_k == 3 and w_ffn_conv.shape[1] == 3
    alpha = (2.0 * depth) ** 0.25

    w_in_b, w_out_b, w_up_b = w_in.astype(BF16), w_out.astype(BF16), w_up.astype(BF16)
    w_down_b, w_pg_b, w_pe_b = w_down.astype(BF16), w_pg.astype(BF16), w_pe.astype(BF16)
    vec3 = lambda a: a.reshape(depth, 1, a.shape[-1])
    ln1_g3, ln1_b3, ln2_g3, ln2_b3, b_conv3 = map(vec3, (ln1_g, ln1_b, ln2_g, ln2_b, b_ffn_conv))
    cache_kt = cache_k.transpose(0, 1, 3, 4, 2).reshape(depth, n_pool, width, page)
    cache_vt = cache_v.transpose(0, 1, 3, 4, 2).reshape(depth, n_pool, width, page)
    pp = p_prompt.reshape(depth, bsz * seq, -1)
    ps = p_sample.reshape(depth, dbsz * dseq, -1)
    bias_rows = jnp.broadcast_to(sb_bias[:, :, None, None], (depth, heads, dseq, LANES)
                                 ).reshape(depth, heads * dseq, LANES)

    xp = x_prompt.reshape(bsz * seq, d)
    xs = x_sample.reshape(dbsz * dseq, d)
    outs = [[] for _ in range(8)]
    for l in range(depth):
        shared_o = (w_out_b, w_pg_b, w_pe_b, ln1_g3, ln1_b3, l)
        shared_f = (w_up_b, w_ffn_conv, b_conv3, w_down_b, ln2_g3, ln2_b3, l)
        z, q, kt, vt, vb, mix_tail = _in_proj(xp, w_in_b, w_conv_mix, l, seq_len=seq, q_dtype=BF16)
        o = _attn_prompt(q, kt, vb, sb_bias, l, head_dim=head_dim)
        x1b, r = _out_proj(z, o, xp, pp, *shared_o, alpha=alpha)
        xp, ffn_tail = _ffn(x1b, r, *shared_f, seq_len=seq)
        outs[0].append(kt.reshape(bsz, heads, head_dim, seq))
        outs[1].append(vt.reshape(bsz, heads, head_dim, seq))
        outs[2].append(_last_rows(mix_tail, bsz, conv_k - 1))
        outs[3].append(_last_rows(ffn_tail, bsz, conv_k - 1))
        z, q, k, v, mix_tail = _in_proj(xs, w_in_b, w_conv_mix, l, seq_len=dseq, q_dtype=F32,
                                        state=_state_rows(state_conv_mix[l], dseq))
        o = _attn_sample(q, k, v, cache_kt, cache_vt, page_table, bias_rows[l], l,
                         seq_len=dseq, head_dim=head_dim)
        x1b, r = _out_proj(z, o, xs, ps, *shared_o, alpha=alpha)
        xs, ffn_tail = _ffn(x1b, r, *shared_f, seq_len=dseq, state=_state_rows(state_conv_ffn[l], dseq))
        outs[4].append(k.reshape(dbsz, dseq, heads, head_dim))
        outs[5].append(v.reshape(dbsz, dseq, heads, head_dim))
        outs[6].append(mix_tail.reshape(dbsz, dseq, -1)[:, dseq - (conv_k - 1):])
        outs[7].append(ffn_tail.reshape(dbsz, dseq, -1)[:, dseq - (conv_k - 1):])
    stacked = [jnp.stack(o) for o in outs]
    stacked[0] = stacked[0].transpose(0, 1, 4, 2, 3)
    stacked[1] = stacked[1].transpose(0, 1, 4, 2, 3)
    return (xp.reshape(bsz, seq, d), xs.reshape(dbsz, dseq, d), *stacked)
```

```python
import functools

import jax
import jax.numpy as jnp
from jax import lax
from jax.experimental import pallas as pl
from jax.experimental.pallas import tpu as pltpu

BF16 = jnp.bfloat16
F32 = jnp.float32

LN_EPS = 1e-5
LANES = 128
SUBLANES = 8
VMEM_LIMIT = 56 * 1024 * 1024
FFN_CHUNK = 256

_dot = functools.partial(jnp.dot, preferred_element_type=F32)


def _dot_nt(a, b):
    return lax.dot_general(a, b, (((1,), (1,)), ((), ())), preferred_element_type=F32)


def _params(n_grid):
    return pltpu.CompilerParams(dimension_semantics=("arbitrary",) * n_grid,
                                vmem_limit_bytes=VMEM_LIMIT)


def _shift_rows_carry(u, prev8, shift):
    p = pltpu.roll(u, shift, axis=0)
    row8 = lax.broadcasted_iota(jnp.int32, prev8.shape, 0)
    top = jnp.where(row8 < shift, pltpu.roll(prev8, shift, axis=0), p[0:SUBLANES])
    return jnp.concatenate([top, p[SUBLANES:]], axis=0)


def _conv3_carry(u, prev8, w):
    return (w[0:1] * _shift_rows_carry(u, prev8, 2) + w[1:2] * _shift_rows_carry(u, prev8, 1)
            + w[2:3] * u)


def _conv3_state(u, e, w, seq_len):
    tm = u.shape[0]
    pos = lax.broadcasted_iota(jnp.int32, u.shape, 0) % seq_len
    p1 = jnp.where(pos >= 1, pltpu.roll(u, 1, axis=0), pltpu.roll(e, tm - 1, axis=0))
    p2 = jnp.where(pos >= 2, pltpu.roll(u, 2, axis=0), e)
    return w[0:1] * p2 + w[1:2] * p1 + w[2:3] * u


def _layer_norm(y, g, b):
    mu = jnp.mean(y, axis=-1, keepdims=True)
    d = y - mu
    var = jnp.mean(d * d, axis=-1, keepdims=True)
    return d * lax.rsqrt(var + LN_EPS) * g + b


def _gelu_tanh(x):
    return x * (0.5 * (1.0 + jnp.tanh(0.7978845608028654 * (x + 0.044715 * (x * x * x)))))


LOG2E = 1.4426950408889634


def _log2_1m_beta(z2):
    nz = -z2
    return jnp.minimum(nz, 0.0) - jnp.log(1.0 + jnp.exp2(jnp.minimum(z2, nz))) * LOG2E


def _suffix_sums(lsn, tri2):
    hi = lsn.astype(BF16)
    lo = (lsn - hi.astype(F32)).astype(BF16)
    return _dot(jnp.concatenate([hi, lo], axis=1), tri2)


def _tri2(n):
    r = lax.broadcasted_iota(jnp.int32, (2 * n, n), 0)
    c = lax.broadcasted_iota(jnp.int32, (2 * n, n), 1)
    r = jnp.where(r >= n, r - n, r)
    return jnp.where(r >= c, 1.0, 0.0).astype(BF16)


def _in_proj_kernel(*refs, seq_len, carry_mode):
    if carry_mode:
        (x_ref, wb_ref, wc_ref, wx_ref, wq_ref, wk_ref, wv_ref, wconv_ref,
         z_ref, q_ref, kt_ref, vt_ref, vb_ref, tail_ref, carry_ref) = refs
    else:
        (x_ref, wb_ref, wc_ref, wx_ref, wq_ref, wk_ref, wv_ref, wconv_ref, e_ref,
         z_ref, q_ref, k_ref, v_ref, tail_ref) = refs
    tm = x_ref.shape[0]
    xb = x_ref[...].astype(BF16)
    u = _dot(xb, wc_ref[...]) * _dot(xb, wx_ref[...])
    if carry_mode:
        i = pl.program_id(1)

        @pl.when(i % (seq_len // tm) == 0)
        def _():
            carry_ref[...] = jnp.zeros_like(carry_ref)

        y = _conv3_carry(u, carry_ref[...], wconv_ref[...])
        carry_ref[...] = u[tm - SUBLANES:, :]
        tail_ref[...] = u[tm - SUBLANES:, :]
    else:
        y = _conv3_state(u, e_ref[...], wconv_ref[...], seq_len)
        tail_ref[...] = u
    z_ref[...] = (_dot(xb, wb_ref[...]) * y).astype(z_ref.dtype)
    q_ref[...] = _dot(xb, wq_ref[...]).astype(q_ref.dtype)
    k = _dot(xb, wk_ref[...])
    v = _dot(xb, wv_ref[...])
    if carry_mode:
        kt_ref[...] = k.T
        vt_ref[...] = v.T
        vb_ref[...] = v.astype(BF16)
    else:
        k_ref[...] = k
        v_ref[...] = v


def _in_proj(x, w_in_b, w_conv, layer, *, seq_len, q_dtype, state=None):
    m, d = x.shape
    c = w_conv.shape[-1]
    carry_mode = state is None
    tm = min(512, m)
    tn = 512
    nb = c // tn
    batch = m // seq_len
    grid = (nb, m // tm)

    def wspec(s):
        return pl.BlockSpec((None, d, tn), lambda j, i, s=s: (layer, 0, s * nb + j))

    in_specs = [pl.BlockSpec((tm, d), lambda j, i: (i, 0))]
    in_specs += [wspec(s) for s in range(6)]
    in_specs += [pl.BlockSpec((None, w_conv.shape[1], tn), lambda j, i: (layer, 0, j))]
    args = [x] + [w_in_b] * 6 + [w_conv]
    tile = pl.BlockSpec((tm, tn), lambda j, i: (i, j))
    rows = lambda dt: jax.ShapeDtypeStruct((m, c), dt)
    scratch = []
    if carry_mode:
        assert seq_len % tm == 0
        per_seq = seq_len // tm
        tspec = pl.BlockSpec((None, tn, tm), lambda j, i: (i // per_seq, j, i % per_seq))
        tshape = jax.ShapeDtypeStruct((batch, c, seq_len), F32)
        tail_shape = jax.ShapeDtypeStruct((m // tm, SUBLANES, c), F32)
        tail_spec = pl.BlockSpec((None, SUBLANES, tn), lambda j, i: (i, 0, j))
        scratch = [pltpu.VMEM((SUBLANES, tn), F32)]
        out_shape = (rows(BF16), rows(q_dtype), tshape, tshape, rows(BF16), tail_shape)
        out_specs = (tile, tile, tspec, tspec, tile, tail_spec)
    else:
        assert tm == m and seq_len == SUBLANES
        in_specs.append(tile)
        args.append(state)
        out_shape = (rows(BF16), rows(q_dtype), rows(F32), rows(F32), rows(F32))
        out_specs = (tile,) * 5
    return pl.pallas_call(
        functools.partial(_in_proj_kernel, seq_len=seq_len, carry_mode=carry_mode),
        grid=grid, in_specs=in_specs, out_specs=out_specs,
        out_shape=out_shape, scratch_shapes=scratch, compiler_params=_params(2),
        name="in_proj")(*args)


def _attn_prompt_kernel(bias_ref, q_ref, kt_ref, v_ref, o_ref,
                        kb_ref, tri_ref, acc_ref, car_ref, *, layer, tk, head_dim, scale):
    c = pl.program_id(1)
    qi = pl.program_id(2)
    tq, width = q_ref.shape
    heads_per_tile = LANES // head_dim
    n_heads = width // head_dim
    nk = kt_ref.shape[1] // tk
    hs = range(n_heads)
    tile = lambda h: slice((h // heads_per_tile) * LANES, (h // heads_per_tile + 1) * LANES)

    @pl.when(qi == 0)
    def _():
        chan = lax.broadcasted_iota(jnp.int32, (LANES, tk), 0)
        for j in range(nk):
            for h in hs:
                lo = (h % heads_per_tile) * head_dim
                kt = kt_ref[tile(h), j * tk:(j + 1) * tk] * (scale * LOG2E)
                kb_ref[j, h] = jnp.where((chan >= lo) & (chan < lo + head_dim), kt, 0.0).astype(BF16)
        tri_ref[...] = _tri2(tk)

    acc_ref[...] = jnp.zeros_like(acc_ref)
    car_ref[...] = jnp.zeros_like(car_ref)
    qs = [q_ref[:, tile(h)] for h in hs]
    bias2 = [bias_ref[layer, c * n_heads + h] * LOG2E for h in hs]

    def block(j, masked):
        koff = pl.multiple_of(j * tk, tk)
        tri = tri_ref[...]
        zs = [_dot(qs[h], kb_ref[j, h]) + bias2[h] for h in hs]
        lsns = [_log2_1m_beta(z) for z in zs]
        if masked:
            row = lax.broadcasted_iota(jnp.int32, (tq, tk), 0)
            col = lax.broadcasted_iota(jnp.int32, (tq, tk), 1)
            valid = col < row
            lsns = [jnp.where(valid, lsn, 0.0) for lsn in lsns]
        locs = [_suffix_sums(lsn, tri) for lsn in lsns]
        carries = [car_ref[h] for h in hs]
        avs = [jnp.exp2(z + loc + jnp.tile(cr, (1, tk // LANES))) for z, loc, cr in zip(zs, locs, carries)]
        if masked:
            avs = [jnp.where(valid, a, 0.0) for a in avs]
        outs = [_dot(avs[h].astype(BF16), v_ref[pl.ds(koff, tk), tile(h)]) for h in hs]
        for h in hs:
            acc_ref[h] += outs[h]
            car_ref[h] = carries[h] + jnp.broadcast_to(locs[h][:, 0:1], carries[h].shape)

    block(qi, True)

    def body(it, carry):
        block(qi - 1 - it, False)
        return carry

    lax.fori_loop(0, qi, body, 0)

    lane = lax.broadcasted_iota(jnp.int32, (tq, LANES), 1)
    for t in range(width // LANES):
        out = acc_ref[t * heads_per_tile]
        for h in range(1, heads_per_tile):
            out = jnp.where(lane >= h * head_dim, acc_ref[t * heads_per_tile + h], out)
        o_ref[:, t * LANES:(t + 1) * LANES] = out.astype(o_ref.dtype)


def _attn_prompt(q, kt, v, sb_bias, layer, *, head_dim):
    m, w = q.shape
    batch, _, seq_len = kt.shape
    tq = tk = 256
    width = 2 * LANES
    n_heads = width // head_dim
    nq = seq_len // tq
    grid = (batch, w // width, nq)
    qspec = pl.BlockSpec((tq, width), lambda b, c, i: (b * nq + i, c))
    return pl.pallas_call(
        functools.partial(_attn_prompt_kernel, layer=layer, tk=tk, head_dim=head_dim,
                          scale=head_dim ** -0.5),
        grid=grid,
        in_specs=[pl.BlockSpec(memory_space=pltpu.SMEM), qspec,
                  pl.BlockSpec((None, width, seq_len), lambda b, c, i: (b, c, 0)),
                  pl.BlockSpec((seq_len, width), lambda b, c, i: (b, c))],
        out_specs=qspec,
        out_shape=jax.ShapeDtypeStruct((m, w), BF16),
        scratch_shapes=[pltpu.VMEM((seq_len // tk, n_heads, LANES, tk), BF16),
                        pltpu.VMEM((2 * tk, tk), BF16),
                        pltpu.VMEM((n_heads, tq, LANES), F32),
                        pltpu.VMEM((n_heads, tq, LANES), F32)],
        compiler_params=_params(3), name="attn_prompt")(sb_bias, q, kt, v)


def _attn_sample_kernel(pt_ref, bias_ref, q_ref, kn_ref, vn_ref, *refs, heads, head_dim, scale, group):
    del pt_ref
    ck_refs, cv_refs = refs[:group], refs[group:2 * group]
    o_ref, wt_ref, tri_ref, acc_ref, car_ref = refs[2 * group:]
    j = pl.program_id(1)
    t, w = q_ref.shape
    page = ck_refs[0].shape[1]
    rows = heads * t

    def process(ks, vs, new_rows):
        wt, tri, bias2 = wt_ref[...], tri_ref[...], bias_ref[...]
        qk = _dot_nt if new_rows else _dot
        zs = [qk(wt, k.astype(BF16)) * (scale * LOG2E) + bias2 for k in ks]
        lsns = [_log2_1m_beta(z) for z in zs]
        if new_rows:
            row = lax.broadcasted_iota(jnp.int32, (rows, page), 0)
            col = lax.broadcasted_iota(jnp.int32, (rows, page), 1)
            valid = col < row % t
            lsns = [jnp.where(valid, lsn, 0.0) for lsn in lsns]
        locs = [_suffix_sums(lsn, tri) for lsn in lsns]
        carry = car_ref[...]
        avs = []
        for z, loc in zip(zs, locs):
            avs.append(jnp.exp2(z + loc + carry))
            carry = carry + jnp.broadcast_to(loc[:, 0:1], carry.shape)
        car_ref[...] = carry
        if new_rows:
            avs = [jnp.where(valid, a, 0.0) for a in avs]
        a_all = jnp.concatenate([a.astype(BF16) for a in avs], axis=1)
        if new_rows:
            acc_ref[...] += _dot(a_all, jnp.concatenate([v.astype(BF16) for v in vs], axis=0))
        else:
            acc_ref[...] += _dot_nt(a_all, jnp.concatenate([v.astype(BF16) for v in vs], axis=1))

    @pl.when(j == 0)
    def _():
        qt = jnp.tile(q_ref[...], (heads, 1))
        rowh = lax.broadcasted_iota(jnp.int32, (rows, w), 0) // t
        colh = lax.broadcasted_iota(jnp.int32, (rows, w), 1) // head_dim
        wt_ref[...] = jnp.where(rowh == colh, qt, 0.0).astype(BF16)
        tri_ref[...] = _tri2(page)
        acc_ref[...] = jnp.zeros_like(acc_ref)
        car_ref[...] = jnp.zeros_like(car_ref)
        pad = jnp.zeros((page - t, w), F32)
        process([jnp.concatenate([kn_ref[...], pad], axis=0)],
                [jnp.concatenate([vn_ref[...], pad], axis=0)], True)

    @pl.when(j > 0)
    def _():
        process([r[...] for r in ck_refs], [r[...] for r in cv_refs], False)

    @pl.when(j == pl.num_programs(1) - 1)
    def _():
        heads_per_tile = LANES // head_dim
        lane = lax.broadcasted_iota(jnp.int32, (t, LANES), 1)
        for p in range(w // LANES):
            cols = slice(p * LANES, (p + 1) * LANES)
            r0 = p * heads_per_tile * t
            out = acc_ref[r0:r0 + t, cols]
            for h in range(1, heads_per_tile):
                out = jnp.where(lane >= h * head_dim, acc_ref[r0 + h * t:r0 + (h + 1) * t, cols], out)
            o_ref[:, cols] = out


def _attn_sample(q, k_new, v_new, cache_kt, cache_vt, page_table, bias_rows, layer, *, seq_len, head_dim):
    m, w = q.shape
    dbsz, n_pages = page_table.shape
    page = cache_kt.shape[3]
    heads = w // head_dim
    rows = heads * seq_len
    group = 4
    assert page == LANES and seq_len == SUBLANES and n_pages % group == 0
    grid = (dbsz, n_pages // group + 1)
    rowspec = pl.BlockSpec((seq_len, w), lambda b, j, pt: (b, 0))

    def pagespec(g):
        return pl.BlockSpec(
            (None, None, w, page),
            lambda b, j, pt: (layer, pt[b, n_pages - 1 - ((jnp.maximum(j, 1) - 1) * group + g)], 0, 0))

    pagespecs = [pagespec(g) for g in range(group)]
    grid_spec = pltpu.PrefetchScalarGridSpec(
        num_scalar_prefetch=1, grid=grid,
        in_specs=[pl.BlockSpec((rows, LANES), lambda b, j, pt: (0, 0)),
                  rowspec, rowspec, rowspec] + pagespecs + pagespecs,
        out_specs=rowspec,
        scratch_shapes=[pltpu.VMEM((rows, w), BF16),
                        pltpu.VMEM((2 * page, page), BF16),
                        pltpu.VMEM((rows, w), F32),
                        pltpu.VMEM((rows, LANES), F32)])
    return pl.pallas_call(
        functools.partial(_attn_sample_kernel, heads=heads, head_dim=head_dim, scale=head_dim ** -0.5,
                          group=group),
        grid_spec=grid_spec, out_shape=jax.ShapeDtypeStruct((m, w), F32),
        compiler_params=_params(2), name="attn_sample")(
            page_table, bias_rows, q, k_new, v_new, *([cache_kt] * group), *([cache_vt] * group))


def _out_proj_kernel(z_ref, o_ref, x_ref, p_ref, wout_ref, wpg_ref, wpe_ref, g_ref, b_ref,
                     x1b_ref, r_ref, *, alpha):
    mixed = jnp.concatenate([z_ref[...].astype(BF16), o_ref[...].astype(BF16)], axis=1)
    y = alpha * x_ref[...] + _dot(mixed, wout_ref[...])
    x1 = _layer_norm(y, g_ref[...], b_ref[...])
    x1b = x1.astype(BF16)
    x1b_ref[...] = x1b
    gate = 1.0 / (1.0 + jnp.exp(-_dot(x1b, wpg_ref[...])))
    r_ref[...] = alpha * x1 + gate * _dot(p_ref[...].astype(BF16), wpe_ref[...])


def _out_proj(z, o, x, p, w_out_b, w_pg_b, w_pe_b, ln_g, ln_b, layer, *, alpha):
    m, d = x.shape
    c = z.shape[1]
    pdim = p.shape[-1]
    tm = min(256, m)
    rowspec = lambda width: pl.BlockSpec((tm, width), lambda i: (i, 0))
    const = lambda r, cc: pl.BlockSpec((None, r, cc), lambda i: (layer, 0, 0),
                                       pipeline_mode=pl.Buffered(1))
    return pl.pallas_call(
        functools.partial(_out_proj_kernel, alpha=alpha),
        grid=(m // tm,),
        in_specs=[rowspec(c), rowspec(c), rowspec(d),
                  pl.BlockSpec((None, tm, pdim), lambda i: (layer, i, 0)),
                  const(2 * c, d), const(d, d), const(pdim, d), const(1, d), const(1, d)],
        out_specs=(rowspec(d), rowspec(d)),
        out_shape=(jax.ShapeDtypeStruct((m, d), BF16), jax.ShapeDtypeStruct((m, d), F32)),
        compiler_params=_params(1), name="out_proj")(
            z, o, x, p, w_out_b, w_pg_b, w_pe_b, ln_g, ln_b)


def _ffn_kernel(*refs, seq_len, carry_mode):
    if carry_mode:
        (x1b_ref, r_ref, wa_ref, wg_ref, wconv_ref, bconv_ref, wd_ref, g_ref, b_ref,
         out_ref, tail_ref, carry_ref) = refs
    else:
        (x1b_ref, r_ref, wa_ref, wg_ref, wconv_ref, bconv_ref, wd_ref, g_ref, b_ref, e_ref,
         out_ref, tail_ref) = refs
    i = pl.program_id(0)
    f = pl.program_id(1)
    tm = x1b_ref.shape[0]
    tf = wa_ref.shape[1]
    x1b = x1b_ref[...]

    if carry_mode:
        @pl.when(i % (seq_len // tm) == 0)
        def _():
            carry_ref[f] = jnp.zeros(carry_ref.shape[1:], F32)

    @pl.when(f == 0)
    def _():
        out_ref[...] = r_ref[...]

    acc = out_ref[...]
    for cols in (slice(c, c + FFN_CHUNK) for c in range(0, tf, FFN_CHUNK)):
        a = _dot(x1b, wa_ref[:, cols])
        gate = _dot(x1b, wg_ref[:, cols])
        if carry_mode:
            a_c = _conv3_carry(a, carry_ref[f, :, cols], wconv_ref[:, cols])
            carry_ref[f, :, cols] = a[tm - SUBLANES:, :]
            tail_ref[:, cols] = a[tm - SUBLANES:, :]
        else:
            a_c = _conv3_state(a, e_ref[:, cols], wconv_ref[:, cols], seq_len)
            tail_ref[:, cols] = a
        h = (_gelu_tanh(a_c + bconv_ref[:, cols]) * gate).astype(BF16)
        acc = acc + _dot(h, wd_ref[cols, :])
    out_ref[...] = acc

    @pl.when(f == pl.num_programs(1) - 1)
    def _():
        out_ref[...] = _layer_norm(out_ref[...], g_ref[...], b_ref[...])


def _ffn(x1b, r, w_up_b, w_conv, b_conv, w_down_b, ln_g, ln_b, layer, *, seq_len, state=None):
    m, d = r.shape
    dff = w_conv.shape[-1]
    carry_mode = state is None
    tm = min(512, m)
    tf = 512
    nf = dff // tf
    assert dff % tf == 0
    rowspec = pl.BlockSpec((tm, d), lambda i, f: (i, 0))
    vec = lambda rows, width: pl.BlockSpec((None, rows, width), lambda i, f: (layer, 0, 0))
    in_specs = [rowspec, rowspec,
                pl.BlockSpec((None, d, tf), lambda i, f: (layer, 0, f)),
                pl.BlockSpec((None, d, tf), lambda i, f: (layer, 0, nf + f)),
                pl.BlockSpec((None, w_conv.shape[1], tf), lambda i, f: (layer, 0, f)),
                pl.BlockSpec((None, 1, tf), lambda i, f: (layer, 0, f)),
                pl.BlockSpec((None, tf, d), lambda i, f: (layer, f, 0)),
                vec(1, d), vec(1, d)]
    args = [x1b, r, w_up_b, w_up_b, w_conv, b_conv, w_down_b, ln_g, ln_b]
    scratch = []
    if carry_mode:
        assert seq_len % tm == 0
        tail_shape = jax.ShapeDtypeStruct((m // tm, SUBLANES, dff), F32)
        tail_spec = pl.BlockSpec((None, SUBLANES, tf), lambda i, f: (i, 0, f))
        scratch = [pltpu.VMEM((nf, SUBLANES, tf), F32)]
    else:
        assert tm == m and seq_len == SUBLANES
        tile = pl.BlockSpec((tm, tf), lambda i, f: (i, f))
        in_specs.append(tile)
        args.append(state)
        tail_shape = jax.ShapeDtypeStruct((m, dff), F32)
        tail_spec = tile
    return pl.pallas_call(
        functools.partial(_ffn_kernel, seq_len=seq_len, carry_mode=carry_mode),
        grid=(m // tm, nf), in_specs=in_specs, out_specs=(rowspec, tail_spec),
        out_shape=(jax.ShapeDtypeStruct((m, d), F32), tail_shape),
        scratch_shapes=scratch, compiler_params=_params(2), name="ffn")(*args)


def _state_rows(state, seq_len):
    b, k1, c = state.shape
    return jnp.pad(state, ((0, 0), (0, seq_len - k1), (0, 0))).reshape(b * seq_len, c)


def _last_rows(tails, batch, n):
    per_seq = tails.shape[0] // batch
    return tails[per_seq - 1::per_seq, SUBLANES - n:]


def kernel(x_prompt, x_sample, cache_k, cache_v, state_conv_mix, state_conv_ffn, page_table, p_prompt,
           p_sample, w_in, w_conv_mix, sb_bias, w_out, ln1_g, ln1_b, w_up, w_ffn_conv, b_ffn_conv,
           w_down, w_pg, w_pe, ln2_g, ln2_b):
    bsz, seq, d = x_prompt.shape
    dbsz, dseq, _ = x_sample.shape
    depth, n_pool, page, heads, head_dim = cache_k.shape
    width = heads * head_dim
    conv_k = w_conv_mix.shape[1]
    assert conv_k == 3 and w_ffn_conv.shape[1] == 3
    alpha = (2.0 * depth) ** 0.25

    w_in_b, w_out_b, w_up_b = w_in.astype(BF16), w_out.astype(BF16), w_up.astype(BF16)
    w_down_b, w_pg_b, w_pe_b = w_down.astype(BF16), w_pg.astype(BF16), w_pe.astype(BF16)
    vec3 = lambda a: a.reshape(depth, 1, a.shape[-1])
    ln1_g3, ln1_b3, ln2_g3, ln2_b3, b_conv3 = map(vec3, (ln1_g, ln1_b, ln2_g, ln2_b, b_ffn_conv))
    cache_kt = cache_k.transpose(0, 1, 3, 4, 2).reshape(depth, n_pool, width, page)
    cache_vt = cache_v.transpose(0, 1, 3, 4, 2).reshape(depth, n_pool, width, page)
    pp = p_prompt.reshape(depth, bsz * seq, -1)
    ps = p_sample.reshape(depth, dbsz * dseq, -1)
    bias_rows = jnp.broadcast_to((sb_bias * LOG2E)[:, :, None, None], (depth, heads, dseq, LANES)
                                 ).reshape(depth, heads * dseq, LANES)

    xp = x_prompt.reshape(bsz * seq, d)
    xs = x_sample.reshape(dbsz * dseq, d)
    outs = [[] for _ in range(8)]
    for l in range(depth):
        shared_o = (w_out_b, w_pg_b, w_pe_b, ln1_g3, ln1_b3, l)
        shared_f = (w_up_b, w_ffn_conv, b_conv3, w_down_b, ln2_g3, ln2_b3, l)
        z, q, kt, vt, vb, mix_tail = _in_proj(xp, w_in_b, w_conv_mix, l, seq_len=seq, q_dtype=BF16)
        o = _attn_prompt(q, kt, vb, sb_bias, l, head_dim=head_dim)
        x1b, r = _out_proj(z, o, xp, pp, *shared_o, alpha=alpha)
        xp, ffn_tail = _ffn(x1b, r, *shared_f, seq_len=seq)
        outs[0].append(kt.reshape(bsz, heads, head_dim, seq))
        outs[1].append(vt.reshape(bsz, heads, head_dim, seq))
        outs[2].append(_last_rows(mix_tail, bsz, conv_k - 1))
        outs[3].append(_last_rows(ffn_tail, bsz, conv_k - 1))
        z, q, k, v, mix_tail = _in_proj(xs, w_in_b, w_conv_mix, l, seq_len=dseq, q_dtype=F32,
                                        state=_state_rows(state_conv_mix[l], dseq))
        o = _attn_sample(q, k, v, cache_kt, cache_vt, page_table, bias_rows[l], l,
                         seq_len=dseq, head_dim=head_dim)
        x1b, r = _out_proj(z, o, xs, ps, *shared_o, alpha=alpha)
        xs, ffn_tail = _ffn(x1b, r, *shared_f, seq_len=dseq, state=_state_rows(state_conv_ffn[l], dseq))
        outs[4].append(k.reshape(dbsz, dseq, heads, head_dim))
        outs[5].append(v.reshape(dbsz, dseq, heads, head_dim))
        outs[6].append(mix_tail.reshape(dbsz, dseq, -1)[:, dseq - (conv_k - 1):])
        outs[7].append(ffn_tail.reshape(dbsz, dseq, -1)[:, dseq - (conv_k - 1):])
    stacked = [jnp.stack(o) for o in outs]
    stacked[0] = stacked[0].transpose(0, 1, 4, 2, 3)
    stacked[1] = stacked[1].transpose(0, 1, 4, 2, 3)
    return (xp.reshape(bsz, seq, d), xs.reshape(dbsz, dseq, d), *stacked)
```

```python
import functools

import jax
import jax.numpy as jnp
from jax import lax
from jax.experimental import pallas as pl
from jax.experimental.pallas import tpu as pltpu

BF16 = jnp.bfloat16
F32 = jnp.float32

LN_EPS = 1e-5
LANES = 128
SUBLANES = 8
VMEM_LIMIT = 56 * 1024 * 1024
OUT_ROWS = 512
OUT_ROW_CHUNK = 256
FFN_COLS = 256
FFN_ROWS = 1024
FFN_ROW_CHUNK = 512

_dot = functools.partial(jnp.dot, preferred_element_type=F32)


def _dot_nt(a, b):
    return lax.dot_general(a, b, (((1,), (1,)), ((), ())), preferred_element_type=F32)


def _params(n_grid):
    return pltpu.CompilerParams(dimension_semantics=("arbitrary",) * n_grid,
                                vmem_limit_bytes=VMEM_LIMIT)


def _shift_rows_carry(u, prev8, shift):
    p = pltpu.roll(u, shift, axis=0)
    row8 = lax.broadcasted_iota(jnp.int32, prev8.shape, 0)
    top = jnp.where(row8 < shift, pltpu.roll(prev8, shift, axis=0), p[0:SUBLANES])
    return jnp.concatenate([top, p[SUBLANES:]], axis=0)


def _conv3_carry(u, prev8, w):
    return (w[0:1] * _shift_rows_carry(u, prev8, 2) + w[1:2] * _shift_rows_carry(u, prev8, 1)
            + w[2:3] * u)


def _conv3_state(u, e, w, seq_len):
    tm = u.shape[0]
    pos = lax.broadcasted_iota(jnp.int32, u.shape, 0) % seq_len
    p1 = jnp.where(pos >= 1, pltpu.roll(u, 1, axis=0), pltpu.roll(e, tm - 1, axis=0))
    p2 = jnp.where(pos >= 2, pltpu.roll(u, 2, axis=0), e)
    return w[0:1] * p2 + w[1:2] * p1 + w[2:3] * u


def _layer_norm(y, g, b):
    mu = jnp.mean(y, axis=-1, keepdims=True)
    d = y - mu
    var = jnp.mean(d * d, axis=-1, keepdims=True)
    return d * lax.rsqrt(var + LN_EPS) * g + b


def _gelu_tanh(x):
    return x * (0.5 * (1.0 + jnp.tanh(0.7978845608028654 * (x + 0.044715 * (x * x * x)))))


LOG2E = 1.4426950408889634


def _log2_1m_beta(z2):
    nz = -z2
    return jnp.minimum(nz, 0.0) - jnp.log(1.0 + jnp.exp2(jnp.minimum(z2, nz))) * LOG2E


def _suffix_sums(lsn, tri2):
    hi = lsn.astype(BF16)
    lo = (lsn - hi.astype(F32)).astype(BF16)
    return _dot(jnp.concatenate([hi, lo], axis=1), tri2)


def _tri2(n):
    r = lax.broadcasted_iota(jnp.int32, (2 * n, n), 0)
    c = lax.broadcasted_iota(jnp.int32, (2 * n, n), 1)
    r = jnp.where(r >= n, r - n, r)
    return jnp.where(r >= c, 1.0, 0.0).astype(BF16)


def _in_proj_kernel(*refs, seq_len, carry_mode):
    if carry_mode:
        (x_ref, wb_ref, wc_ref, wx_ref, wq_ref, wk_ref, wv_ref, wconv_ref,
         z_ref, q_ref, kt_ref, vt_ref, vb_ref, tail_ref, carry_ref) = refs
    else:
        (x_ref, wb_ref, wc_ref, wx_ref, wq_ref, wk_ref, wv_ref, wconv_ref, e_ref,
         z_ref, q_ref, k_ref, v_ref, tail_ref) = refs
    tm = x_ref.shape[0]
    xb = x_ref[...].astype(BF16)
    u = _dot(xb, wc_ref[...]) * _dot(xb, wx_ref[...])
    if carry_mode:
        i = pl.program_id(1)

        @pl.when(i % (seq_len // tm) == 0)
        def _():
            carry_ref[...] = jnp.zeros_like(carry_ref)

        y = _conv3_carry(u, carry_ref[...], wconv_ref[...])
        carry_ref[...] = u[tm - SUBLANES:, :]
        tail_ref[...] = u[tm - SUBLANES:, :]
    else:
        y = _conv3_state(u, e_ref[...], wconv_ref[...], seq_len)
        tail_ref[...] = u
    z_ref[...] = (_dot(xb, wb_ref[...]) * y).astype(z_ref.dtype)
    q_ref[...] = _dot(xb, wq_ref[...]).astype(q_ref.dtype)
    k = _dot(xb, wk_ref[...])
    v = _dot(xb, wv_ref[...])
    if carry_mode:
        kt_ref[...] = k.T
        vt_ref[...] = v.T
        vb_ref[...] = v.astype(BF16)
    else:
        k_ref[...] = k
        v_ref[...] = v


def _in_proj(x, w_in_b, w_conv, layer, *, seq_len, q_dtype, state=None):
    m, d = x.shape
    c = w_conv.shape[-1]
    carry_mode = state is None
    tm = min(512, m)
    tn = 512
    nb = c // tn
    batch = m // seq_len
    grid = (nb, m // tm)

    def wspec(s):
        return pl.BlockSpec((None, d, tn), lambda j, i, s=s: (layer, 0, s * nb + j))

    in_specs = [pl.BlockSpec((tm, d), lambda j, i: (i, 0))]
    in_specs += [wspec(s) for s in range(6)]
    in_specs += [pl.BlockSpec((None, w_conv.shape[1], tn), lambda j, i: (layer, 0, j))]
    args = [x] + [w_in_b] * 6 + [w_conv]
    tile = pl.BlockSpec((tm, tn), lambda j, i: (i, j))
    rows = lambda dt: jax.ShapeDtypeStruct((m, c), dt)
    scratch = []
    if carry_mode:
        assert seq_len % tm == 0
        per_seq = seq_len // tm
        tspec = pl.BlockSpec((None, tn, tm), lambda j, i: (i // per_seq, j, i % per_seq))
        tshape = jax.ShapeDtypeStruct((batch, c, seq_len), F32)
        tail_shape = jax.ShapeDtypeStruct((m // tm, SUBLANES, c), F32)
        tail_spec = pl.BlockSpec((None, SUBLANES, tn), lambda j, i: (i, 0, j))
        scratch = [pltpu.VMEM((SUBLANES, tn), F32)]
        out_shape = (rows(BF16), rows(q_dtype), tshape, tshape, rows(BF16), tail_shape)
        out_specs = (tile, tile, tspec, tspec, tile, tail_spec)
    else:
        assert tm == m and seq_len == SUBLANES
        in_specs.append(tile)
        args.append(state)
        out_shape = (rows(BF16), rows(q_dtype), rows(F32), rows(F32), rows(F32))
        out_specs = (tile,) * 5
    return pl.pallas_call(
        functools.partial(_in_proj_kernel, seq_len=seq_len, carry_mode=carry_mode),
        grid=grid, in_specs=in_specs, out_specs=out_specs,
        out_shape=out_shape, scratch_shapes=scratch, compiler_params=_params(2),
        name="in_proj")(*args)


def _attn_prompt_kernel(bias_ref, q_ref, kt_ref, v_ref, o_ref,
                        kb_ref, tri_ref, acc_ref, car_ref, s_ref, a_ref, *, layer, tk, head_dim, scale):
    c = pl.program_id(1)
    qi = pl.program_id(2)
    tq, width = q_ref.shape
    heads_per_tile = LANES // head_dim
    n_heads = width // head_dim
    nk = kt_ref.shape[1] // tk
    hs = range(n_heads)
    tile = lambda h: slice((h // heads_per_tile) * LANES, (h // heads_per_tile + 1) * LANES)

    @pl.when(qi == 0)
    def _():
        chan = lax.broadcasted_iota(jnp.int32, (LANES, tk), 0)
        for j in range(nk):
            for h in hs:
                lo = (h % heads_per_tile) * head_dim
                kt = kt_ref[tile(h), j * tk:(j + 1) * tk] * (scale * LOG2E)
                kb_ref[j, h] = jnp.where((chan >= lo) & (chan < lo + head_dim), kt, 0.0).astype(BF16)
        tri_ref[...] = _tri2(tk)

    acc_ref[...] = jnp.zeros_like(acc_ref)
    qs = [q_ref[:, tile(h)] for h in hs]
    bias2 = [bias_ref[layer, c * n_heads + h] * LOG2E for h in hs]

    def logits(j):
        return [_dot(qs[h], kb_ref[j, h]) + bias2[h] for h in hs]

    tri = tri_ref[...]
    jn = jnp.maximum(qi - 1, 0)
    row = lax.broadcasted_iota(jnp.int32, (tq, tk), 0)
    col = lax.broadcasted_iota(jnp.int32, (tq, tk), 1)
    valid = col < row
    z_diag = logits(qi)
    z_next, locs = [], []
    for h in hs:
        lsn = jnp.where(valid, _log2_1m_beta(z_diag[h]), 0.0)
        z_next.append(_dot(qs[h], kb_ref[jn, h]) + bias2[h])
        locs.append(_suffix_sums(lsn, tri))
    for h in hs:
        a = jnp.where(valid, jnp.exp2(z_diag[h] + locs[h]), 0.0)
        car_ref[h] = jnp.broadcast_to(locs[h][:, 0:1], (tq, LANES))
        a_ref[h] = a.astype(BF16)
        s_ref[h] = z_next[h]

    def body(it, j_prev):
        j = qi - 1 - it
        jn = jnp.maximum(j - 1, 0)
        koff = pl.multiple_of(j_prev * tk, tk)
        tri = tri_ref[...]
        n = len(hs)
        locs = [None] * n

        def apply_prev(h):
            acc_ref[h] += _dot(a_ref[h], v_ref[pl.ds(koff, tk), tile(h)])

        def finish(h):
            carry = car_ref[h]
            a = jnp.exp2(s_ref[h] + locs[h] + jnp.tile(carry, (1, tk // LANES)))
            car_ref[h] = carry + jnp.broadcast_to(locs[h][:, 0:1], carry.shape)
            a_ref[h] = a.astype(BF16)

        apply_prev(0)
        for h in hs:
            lsn = _log2_1m_beta(s_ref[h])
            if h + 1 < n:
                apply_prev(h + 1)
            locs[h] = _suffix_sums(lsn, tri)
        for h in hs:
            z_next = _dot(qs[h], kb_ref[jn, h]) + bias2[h]
            finish(h)
            s_ref[h] = z_next
        return j

    j_last = lax.fori_loop(0, qi, body, qi)
    koff = pl.multiple_of(j_last * tk, tk)
    outs = [_dot(a_ref[h], v_ref[pl.ds(koff, tk), tile(h)]) for h in hs]

    lane = lax.broadcasted_iota(jnp.int32, (tq, LANES), 1)
    totals = [acc_ref[h] + outs[h] for h in hs]
    for t in range(width // LANES):
        out = totals[t * heads_per_tile]
        for h in range(1, heads_per_tile):
            out = jnp.where(lane >= h * head_dim, totals[t * heads_per_tile + h], out)
        o_ref[:, t * LANES:(t + 1) * LANES] = out.astype(o_ref.dtype)


def _attn_prompt(q, kt, v, sb_bias, layer, *, head_dim):
    m, w = q.shape
    batch, _, seq_len = kt.shape
    tq = tk = 256
    width = 2 * LANES
    n_heads = width // head_dim
    nq = seq_len // tq
    grid = (batch, w // width, nq)
    qspec = pl.BlockSpec((tq, width), lambda b, c, i: (b * nq + i, c))
    return pl.pallas_call(
        functools.partial(_attn_prompt_kernel, layer=layer, tk=tk, head_dim=head_dim,
                          scale=head_dim ** -0.5),
        grid=grid,
        in_specs=[pl.BlockSpec(memory_space=pltpu.SMEM), qspec,
                  pl.BlockSpec((None, width, seq_len), lambda b, c, i: (b, c, 0)),
                  pl.BlockSpec((seq_len, width), lambda b, c, i: (b, c))],
        out_specs=qspec,
        out_shape=jax.ShapeDtypeStruct((m, w), BF16),
        scratch_shapes=[pltpu.VMEM((seq_len // tk, n_heads, LANES, tk), BF16),
                        pltpu.VMEM((2 * tk, tk), BF16),
                        pltpu.VMEM((n_heads, tq, LANES), F32),
                        pltpu.VMEM((n_heads, tq, LANES), F32),
                        pltpu.VMEM((n_heads, tq, tk), F32),
                        pltpu.VMEM((n_heads, tq, tk), BF16)],
        compiler_params=_params(3), name="attn_prompt")(sb_bias, q, kt, v)


def _attn_sample_kernel(pt_ref, bias_ref, q_ref, kn_ref, vn_ref, *refs, heads, head_dim, scale, group):
    del pt_ref
    ck_refs, cv_refs = refs[:group], refs[group:2 * group]
    o_ref, wt_ref, tri_ref, acc_ref, car_ref = refs[2 * group:]
    j = pl.program_id(1)
    t, w = q_ref.shape
    page = ck_refs[0].shape[1]
    rows = heads * t

    def process(ks, vs, new_rows):
        wt, tri, bias2 = wt_ref[...], tri_ref[...], bias_ref[...]
        qk = _dot_nt if new_rows else _dot
        zs = [qk(wt, k.astype(BF16)) * (scale * LOG2E) + bias2 for k in ks]
        lsns = [_log2_1m_beta(z) for z in zs]
        if new_rows:
            row = lax.broadcasted_iota(jnp.int32, (rows, page), 0)
            col = lax.broadcasted_iota(jnp.int32, (rows, page), 1)
            valid = col < row % t
            lsns = [jnp.where(valid, lsn, 0.0) for lsn in lsns]
        locs = [_suffix_sums(lsn, tri) for lsn in lsns]
        carry = car_ref[...]
        avs = []
        for z, loc in zip(zs, locs):
            avs.append(jnp.exp2(z + loc + carry))
            carry = carry + jnp.broadcast_to(loc[:, 0:1], carry.shape)
        car_ref[...] = carry
        if new_rows:
            avs = [jnp.where(valid, a, 0.0) for a in avs]
        a_all = jnp.concatenate([a.astype(BF16) for a in avs], axis=1)
        if new_rows:
            acc_ref[...] += _dot(a_all, jnp.concatenate([v.astype(BF16) for v in vs], axis=0))
        else:
            acc_ref[...] += _dot_nt(a_all, jnp.concatenate([v.astype(BF16) for v in vs], axis=1))

    @pl.when(j == 0)
    def _():
        qt = jnp.tile(q_ref[...], (heads, 1))
        rowh = lax.broadcasted_iota(jnp.int32, (rows, w), 0) // t
        colh = lax.broadcasted_iota(jnp.int32, (rows, w), 1) // head_dim
        wt_ref[...] = jnp.where(rowh == colh, qt, 0.0).astype(BF16)
        tri_ref[...] = _tri2(page)
        acc_ref[...] = jnp.zeros_like(acc_ref)
        car_ref[...] = jnp.zeros_like(car_ref)
        pad = jnp.zeros((page - t, w), F32)
        process([jnp.concatenate([kn_ref[...], pad], axis=0)],
                [jnp.concatenate([vn_ref[...], pad], axis=0)], True)

    @pl.when(j > 0)
    def _():
        process([r[...] for r in ck_refs], [r[...] for r in cv_refs], False)

    @pl.when(j == pl.num_programs(1) - 1)
    def _():
        heads_per_tile = LANES // head_dim
        lane = lax.broadcasted_iota(jnp.int32, (t, LANES), 1)
        for p in range(w // LANES):
            cols = slice(p * LANES, (p + 1) * LANES)
            r0 = p * heads_per_tile * t
            out = acc_ref[r0:r0 + t, cols]
            for h in range(1, heads_per_tile):
                out = jnp.where(lane >= h * head_dim, acc_ref[r0 + h * t:r0 + (h + 1) * t, cols], out)
            o_ref[:, cols] = out


def _attn_sample(q, k_new, v_new, cache_kt, cache_vt, page_table, bias_rows, layer, *, seq_len, head_dim):
    m, w = q.shape
    dbsz, n_pages = page_table.shape
    page = cache_kt.shape[3]
    heads = w // head_dim
    rows = heads * seq_len
    group = 8
    assert page == LANES and seq_len == SUBLANES and n_pages % group == 0
    grid = (dbsz, n_pages // group + 1)
    rowspec = pl.BlockSpec((seq_len, w), lambda b, j, pt: (b, 0))

    def pagespec(g):
        return pl.BlockSpec(
            (None, None, w, page),
            lambda b, j, pt: (layer, pt[b, n_pages - 1 - ((jnp.maximum(j, 1) - 1) * group + g)], 0, 0))

    pagespecs = [pagespec(g) for g in range(group)]
    grid_spec = pltpu.PrefetchScalarGridSpec(
        num_scalar_prefetch=1, grid=grid,
        in_specs=[pl.BlockSpec((rows, LANES), lambda b, j, pt: (0, 0)),
                  rowspec, rowspec, rowspec] + pagespecs + pagespecs,
        out_specs=rowspec,
        scratch_shapes=[pltpu.VMEM((rows, w), BF16),
                        pltpu.VMEM((2 * page, page), BF16),
                        pltpu.VMEM((rows, w), F32),
                        pltpu.VMEM((rows, LANES), F32)])
    return pl.pallas_call(
        functools.partial(_attn_sample_kernel, heads=heads, head_dim=head_dim, scale=head_dim ** -0.5,
                          group=group),
        grid_spec=grid_spec, out_shape=jax.ShapeDtypeStruct((m, w), F32),
        compiler_params=_params(2), name="attn_sample")(
            page_table, bias_rows, q, k_new, v_new, *([cache_kt] * group), *([cache_vt] * group))


def _out_proj_kernel(z_ref, o_ref, x_ref, p_ref, wout_ref, wpg_ref, wpe_ref, g_ref, b_ref,
                     x1b_ref, r_ref, *, alpha, row_chunk):
    chunks = [slice(r0, r0 + row_chunk) for r0 in range(0, x_ref.shape[0], row_chunk)]
    g, b = g_ref[...], b_ref[...]
    mixed = [jnp.concatenate([z_ref[rows, :].astype(BF16), o_ref[rows, :].astype(BF16)], axis=1)
             for rows in chunks]
    ys = [alpha * x_ref[rows, :] + _dot(mx, wout_ref[...]) for rows, mx in zip(chunks, mixed)]
    x1s = [_layer_norm(y, g, b) for y in ys]
    x1bs = [x1.astype(BF16) for x1 in x1s]
    logits = [_dot(x1b, wpg_ref[...]) for x1b in x1bs]
    pes = [_dot(p_ref[rows, :].astype(BF16), wpe_ref[...]) for rows in chunks]
    for rows, x1, x1b, lg, pe in zip(chunks, x1s, x1bs, logits, pes):
        x1b_ref[rows, :] = x1b
        r_ref[rows, :] = alpha * x1 + pe * (1.0 / (1.0 + jnp.exp(-lg)))


def _out_proj(z, o, x, p, w_out_b, w_pg_b, w_pe_b, ln_g, ln_b, layer, *, alpha):
    m, d = x.shape
    c = z.shape[1]
    pdim = p.shape[-1]
    tm = min(OUT_ROWS, m)
    rowspec = lambda width: pl.BlockSpec((tm, width), lambda i: (i, 0))
    const = lambda r, cc: pl.BlockSpec((None, r, cc), lambda i: (layer, 0, 0),
                                       pipeline_mode=pl.Buffered(1))
    return pl.pallas_call(
        functools.partial(_out_proj_kernel, alpha=alpha, row_chunk=min(OUT_ROW_CHUNK, tm)),
        grid=(m // tm,),
        in_specs=[rowspec(c), rowspec(c), rowspec(d),
                  pl.BlockSpec((None, tm, pdim), lambda i: (layer, i, 0)),
                  const(2 * c, d), const(d, d), const(pdim, d), const(1, d), const(1, d)],
        out_specs=(rowspec(d), rowspec(d)),
        out_shape=(jax.ShapeDtypeStruct((m, d), BF16), jax.ShapeDtypeStruct((m, d), F32)),
        compiler_params=_params(1), name="out_proj")(
            z, o, x, p, w_out_b, w_pg_b, w_pe_b, ln_g, ln_b)


def _ffn_kernel(*refs, seq_len, carry_mode, row_chunk):
    if carry_mode:
        (x1b_ref, r_ref, wa_ref, wg_ref, wconv_ref, bconv_ref, wd_ref, g_ref, b_ref,
         out_ref, tail_ref, carry_ref) = refs
    else:
        (x1b_ref, r_ref, wa_ref, wg_ref, wconv_ref, bconv_ref, wd_ref, g_ref, b_ref, e_ref,
         out_ref, tail_ref) = refs
    i = pl.program_id(0)
    f = pl.program_id(1)
    tm = x1b_ref.shape[0]

    @pl.when(f == 0)
    def _():
        out_ref[...] = r_ref[...]

    if carry_mode:
        @pl.when(i % (seq_len // tm) == 0)
        def _():
            carry_ref[f] = jnp.zeros(carry_ref.shape[1:], F32)

        prev8 = carry_ref[f]
    chunks = [slice(r0, r0 + row_chunk) for r0 in range(0, tm, row_chunk)]
    ups = [(_dot(x1b_ref[rows, :], wa_ref[...]), _dot(x1b_ref[rows, :], wg_ref[...])) for rows in chunks]
    hs = []
    for rows, (a, gate) in zip(chunks, ups):
        if carry_mode:
            a_c = _conv3_carry(a, prev8, wconv_ref[...])
            prev8 = a[row_chunk - SUBLANES:, :]
        else:
            a_c = _conv3_state(a, e_ref[rows, :], wconv_ref[...], seq_len)
            tail_ref[rows, :] = a
        hs.append((_gelu_tanh(a_c + bconv_ref[...]) * gate).astype(BF16))
    for rows, h in zip(chunks, hs):
        out_ref[rows, :] += _dot(h, wd_ref[...])
    if carry_mode:
        carry_ref[f] = prev8
        tail_ref[...] = prev8

    @pl.when(f == pl.num_programs(1) - 1)
    def _():
        out_ref[...] = _layer_norm(out_ref[...], g_ref[...], b_ref[...])


def _ffn(x1b, r, w_up_t, w_conv, b_conv, w_down_b, ln_g, ln_b, layer, *, seq_len, state=None):
    m, d = r.shape
    dff = w_conv.shape[-1]
    carry_mode = state is None
    tf = w_up_t.shape[-1]
    nf = dff // tf
    tm = min(FFN_ROWS, m)
    assert w_up_t.shape[1] == 2 * nf
    rowspec = pl.BlockSpec((tm, d), lambda i, f: (i, 0))
    vec = lambda rows, width: pl.BlockSpec((None, rows, width), lambda i, f: (layer, 0, 0))
    in_specs = [rowspec,
                pl.BlockSpec((tm, d), lambda i, f: (i, 0), pipeline_mode=pl.Buffered(1)),
                pl.BlockSpec((None, None, d, tf), lambda i, f: (layer, f, 0, 0)),
                pl.BlockSpec((None, None, d, tf), lambda i, f: (layer, nf + f, 0, 0)),
                pl.BlockSpec((None, w_conv.shape[1], tf), lambda i, f: (layer, 0, f)),
                pl.BlockSpec((None, 1, tf), lambda i, f: (layer, 0, f)),
                pl.BlockSpec((None, tf, d), lambda i, f: (layer, f, 0)),
                vec(1, d), vec(1, d)]
    args = [x1b, r, w_up_t, w_up_t, w_conv, b_conv, w_down_b, ln_g, ln_b]
    scratch = []
    if carry_mode:
        assert seq_len % tm == 0
        tail_shape = jax.ShapeDtypeStruct((m // tm, SUBLANES, dff), F32)
        tail_spec = pl.BlockSpec((None, SUBLANES, tf), lambda i, f: (i, 0, f))
        scratch = [pltpu.VMEM((nf, SUBLANES, tf), F32)]
    else:
        assert tm == m and seq_len == SUBLANES
        tile = pl.BlockSpec((tm, tf), lambda i, f: (i, f))
        in_specs.append(tile)
        args.append(state)
        tail_shape = jax.ShapeDtypeStruct((m, dff), F32)
        tail_spec = tile
    return pl.pallas_call(
        functools.partial(_ffn_kernel, seq_len=seq_len, carry_mode=carry_mode,
                          row_chunk=min(FFN_ROW_CHUNK, tm)),
        grid=(m // tm, nf), in_specs=in_specs, out_specs=(rowspec, tail_spec),
        out_shape=(jax.ShapeDtypeStruct((m, d), F32), tail_shape),
        scratch_shapes=scratch, compiler_params=_params(2), name="ffn")(*args)


def _state_rows(state, seq_len):
    b, k1, c = state.shape
    return jnp.pad(state, ((0, 0), (0, seq_len - k1), (0, 0))).reshape(b * seq_len, c)


def _last_rows(tails, batch, n):
    per_seq = tails.shape[0] // batch
    return tails[per_seq - 1::per_seq, SUBLANES - n:]


def kernel(x_prompt, x_sample, cache_k, cache_v, state_conv_mix, state_conv_ffn, page_table, p_prompt,
           p_sample, w_in, w_conv_mix, sb_bias, w_out, ln1_g, ln1_b, w_up, w_ffn_conv, b_ffn_conv,
           w_down, w_pg, w_pe, ln2_g, ln2_b):
    bsz, seq, d = x_prompt.shape
    dbsz, dseq, _ = x_sample.shape
    depth, n_pool, page, heads, head_dim = cache_k.shape
    width = heads * head_dim
    conv_k = w_conv_mix.shape[1]
    assert conv_k == 3 and w_ffn_conv.shape[1] == 3
    alpha = (2.0 * depth) ** 0.25

    w_in_b, w_out_b = w_in.astype(BF16), w_out.astype(BF16)
    w_up_t = w_up.reshape(depth, d, -1, FFN_COLS).transpose(0, 2, 1, 3).astype(BF16)
    w_down_b, w_pg_b, w_pe_b = w_down.astype(BF16), w_pg.astype(BF16), w_pe.astype(BF16)
    vec3 = lambda a: a.reshape(depth, 1, a.shape[-1])
    ln1_g3, ln1_b3, ln2_g3, ln2_b3, b_conv3 = map(vec3, (ln1_g, ln1_b, ln2_g, ln2_b, b_ffn_conv))
    cache_kt = cache_k.transpose(0, 1, 3, 4, 2).reshape(depth, n_pool, width, page)
    cache_vt = cache_v.transpose(0, 1, 3, 4, 2).reshape(depth, n_pool, width, page)
    pp = p_prompt.reshape(depth, bsz * seq, -1)
    ps = p_sample.reshape(depth, dbsz * dseq, -1)
    bias_rows = jnp.broadcast_to((sb_bias * LOG2E)[:, :, None, None], (depth, heads, dseq, LANES)
                                 ).reshape(depth, heads * dseq, LANES)

    xp = x_prompt.reshape(bsz * seq, d)
    xs = x_sample.reshape(dbsz * dseq, d)
    outs = [[] for _ in range(8)]
    for l in range(depth):
        shared_o = (w_out_b, w_pg_b, w_pe_b, ln1_g3, ln1_b3, l)
        shared_f = (w_up_t, w_ffn_conv, b_conv3, w_down_b, ln2_g3, ln2_b3, l)
        z, q, kt, vt, vb, mix_tail = _in_proj(xp, w_in_b, w_conv_mix, l, seq_len=seq, q_dtype=BF16)
        o = _attn_prompt(q, kt, vb, sb_bias, l, head_dim=head_dim)
        x1b, r = _out_proj(z, o, xp, pp, *shared_o, alpha=alpha)
        xp, ffn_tail = _ffn(x1b, r, *shared_f, seq_len=seq)
        outs[0].append(kt.reshape(bsz, heads, head_dim, seq))
        outs[1].append(vt.reshape(bsz, heads, head_dim, seq))
        outs[2].append(_last_rows(mix_tail, bsz, conv_k - 1))
        outs[3].append(_last_rows(ffn_tail, bsz, conv_k - 1))
        z, q, k, v, mix_tail = _in_proj(xs, w_in_b, w_conv_mix, l, seq_len=dseq, q_dtype=F32,
                                        state=_state_rows(state_conv_mix[l], dseq))
        o = _attn_sample(q, k, v, cache_kt, cache_vt, page_table, bias_rows[l], l,
                         seq_len=dseq, head_dim=head_dim)
        x1b, r = _out_proj(z, o, xs, ps, *shared_o, alpha=alpha)
        xs, ffn_tail = _ffn(x1b, r, *shared_f, seq_len=dseq, state=_state_rows(state_conv_ffn[l], dseq))
        outs[4].append(k.reshape(dbsz, dseq, heads, head_dim))
        outs[5].append(v.reshape(dbsz, dseq, heads, head_dim))
        outs[6].append(mix_tail.reshape(dbsz, dseq, -1)[:, dseq - (conv_k - 1):])
        outs[7].append(ffn_tail.reshape(dbsz, dseq, -1)[:, dseq - (conv_k - 1):])
    stacked = [jnp.stack(o) for o in outs]
    stacked[0] = stacked[0].transpose(0, 1, 4, 2, 3)
    stacked[1] = stacked[1].transpose(0, 1, 4, 2, 3)
    return (xp.reshape(bsz, seq, d), xs.reshape(dbsz, dseq, d), *stacked)
```

```python
import functools

import jax
import jax.numpy as jnp
from jax import lax
from jax.experimental import pallas as pl
from jax.experimental.pallas import tpu as pltpu

BF16 = jnp.bfloat16
F32 = jnp.float32

LN_EPS = 1e-5
LANES = 128
SUBLANES = 8
VMEM_LIMIT = 56 * 1024 * 1024
OUT_ROWS = 512
OUT_ROW_CHUNK = 256
FFN_COLS = 256
FFN_ROWS = 1024
FFN_ROW_CHUNK = 512

_dot = functools.partial(jnp.dot, preferred_element_type=F32)


def _dot_nt(a, b):
    return lax.dot_general(a, b, (((1,), (1,)), ((), ())), preferred_element_type=F32)


def _params(n_grid):
    return pltpu.CompilerParams(dimension_semantics=("arbitrary",) * n_grid,
                                vmem_limit_bytes=VMEM_LIMIT)


def _shift_rows_carry(u, prev8, shift):
    p = pltpu.roll(u, shift, axis=0)
    row8 = lax.broadcasted_iota(jnp.int32, prev8.shape, 0)
    top = jnp.where(row8 < shift, pltpu.roll(prev8, shift, axis=0), p[0:SUBLANES])
    return jnp.concatenate([top, p[SUBLANES:]], axis=0)


def _conv3_carry(u, prev8, w):
    return (w[0:1] * _shift_rows_carry(u, prev8, 2) + w[1:2] * _shift_rows_carry(u, prev8, 1)
            + w[2:3] * u)


def _conv3_state(u, e, w, seq_len):
    tm = u.shape[0]
    pos = lax.broadcasted_iota(jnp.int32, u.shape, 0) % seq_len
    p1 = jnp.where(pos >= 1, pltpu.roll(u, 1, axis=0), pltpu.roll(e, tm - 1, axis=0))
    p2 = jnp.where(pos >= 2, pltpu.roll(u, 2, axis=0), e)
    return w[0:1] * p2 + w[1:2] * p1 + w[2:3] * u


def _layer_norm(y, g, b):
    mu = jnp.mean(y, axis=-1, keepdims=True)
    d = y - mu
    var = jnp.mean(d * d, axis=-1, keepdims=True)
    return d * lax.rsqrt(var + LN_EPS) * g + b


def _gelu_tanh(x):
    return x * (0.5 * (1.0 + jnp.tanh(0.7978845608028654 * (x + 0.044715 * (x * x * x)))))


LOG2E = 1.4426950408889634


def _log2_1m_beta(z2):
    nz = -z2
    return jnp.minimum(nz, 0.0) - jnp.log(1.0 + jnp.exp2(jnp.minimum(z2, nz))) * LOG2E


def _suffix_sums(lsn, tri):
    return _dot(lsn.astype(BF16), tri)


def _tri(n):
    r = lax.broadcasted_iota(jnp.int32, (n, n), 0)
    c = lax.broadcasted_iota(jnp.int32, (n, n), 1)
    return jnp.where(r >= c, 1.0, 0.0).astype(BF16)


def _in_proj_kernel(*refs, seq_len, carry_mode, n_aliased):
    if carry_mode:
        refs = refs[:8] + refs[8 + n_aliased:]
        (x_ref, wb_ref, wc_ref, wx_ref, wq_ref, wk_ref, wv_ref, wconv_ref,
         z_ref, q_ref, kt_ref, vt_ref, vb_ref, tail_ref, carry_ref) = refs
    else:
        (x_ref, wb_ref, wc_ref, wx_ref, wq_ref, wk_ref, wv_ref, wconv_ref, e_ref,
         z_ref, q_ref, k_ref, v_ref, tail_ref) = refs
    tm = x_ref.shape[0]
    xb = x_ref[...].astype(BF16)
    u = _dot(xb, wc_ref[...]) * _dot(xb, wx_ref[...])
    if carry_mode:
        i = pl.program_id(1)

        @pl.when(i % (seq_len // tm) == 0)
        def _():
            carry_ref[...] = jnp.zeros_like(carry_ref)

        y = _conv3_carry(u, carry_ref[...], wconv_ref[...])
        carry_ref[...] = u[tm - SUBLANES:, :]
        tail_ref[...] = u[tm - SUBLANES:, :]
    else:
        y = _conv3_state(u, e_ref[...], wconv_ref[...], seq_len)
        tail_ref[...] = u
    z_ref[...] = (_dot(xb, wb_ref[...]) * y).astype(z_ref.dtype)
    q_ref[...] = _dot(xb, wq_ref[...]).astype(q_ref.dtype)
    k = _dot(xb, wk_ref[...])
    v = _dot(xb, wv_ref[...])
    if carry_mode:
        kt_ref[...] = k.T
        vt_ref[...] = v.T
        vb_ref[...] = v.astype(BF16)
    else:
        k_ref[...] = k
        v_ref[...] = v


def _in_proj(x, w_in_b, w_conv, layer, *, seq_len, q_dtype, state=None, kv_buffers=None):
    m, d = x.shape
    c = w_conv.shape[-1]
    carry_mode = state is None
    tm = min(512, m)
    tn = 512
    nb = c // tn
    batch = m // seq_len
    grid = (nb, m // tm)

    def wspec(s):
        return pl.BlockSpec((None, d, tn), lambda j, i, s=s: (layer, 0, s * nb + j))

    in_specs = [pl.BlockSpec((tm, d), lambda j, i: (i, 0))]
    in_specs += [wspec(s) for s in range(6)]
    in_specs += [pl.BlockSpec((None, w_conv.shape[1], tn), lambda j, i: (layer, 0, j))]
    args = [x] + [w_in_b] * 6 + [w_conv]
    tile = pl.BlockSpec((tm, tn), lambda j, i: (i, j))
    rows = lambda dt: jax.ShapeDtypeStruct((m, c), dt)
    scratch = []
    aliases = {}
    if carry_mode:
        assert seq_len % tm == 0
        per_seq = seq_len // tm
        tspec = pl.BlockSpec((None, None, tn, tm), lambda j, i: (layer, i // per_seq, j, i % per_seq))
        tshape = jax.ShapeDtypeStruct((w_in_b.shape[0], batch, c, seq_len), F32)
        if kv_buffers is not None:
            aliases = {len(args): 2, len(args) + 1: 3}
            in_specs += [pl.BlockSpec(memory_space=pl.ANY)] * 2
            args += list(kv_buffers)
        tail_shape = jax.ShapeDtypeStruct((m // tm, SUBLANES, c), F32)
        tail_spec = pl.BlockSpec((None, SUBLANES, tn), lambda j, i: (i, 0, j))
        scratch = [pltpu.VMEM((SUBLANES, tn), F32)]
        out_shape = (rows(BF16), rows(q_dtype), tshape, tshape, rows(BF16), tail_shape)
        out_specs = (tile, tile, tspec, tspec, tile, tail_spec)
    else:
        assert tm == m and seq_len == SUBLANES
        in_specs.append(tile)
        args.append(state)
        out_shape = (rows(BF16), rows(q_dtype), rows(F32), rows(F32), rows(F32))
        out_specs = (tile,) * 5
    return pl.pallas_call(
        functools.partial(_in_proj_kernel, seq_len=seq_len, carry_mode=carry_mode, n_aliased=len(aliases)),
        grid=grid, in_specs=in_specs, out_specs=out_specs, input_output_aliases=aliases,
        out_shape=out_shape, scratch_shapes=scratch, compiler_params=_params(2),
        name="in_proj")(*args)


def _attn_prompt_kernel(bias_ref, q_ref, kt_ref, v_ref, o_ref,
                        kb_ref, tri_ref, acc_ref, car_ref, s_ref, a_ref, *, layer, tk, head_dim, scale):
    c = pl.program_id(1)
    qi = pl.program_id(2)
    tq, width = q_ref.shape
    heads_per_tile = LANES // head_dim
    n_heads = width // head_dim
    nk = kt_ref.shape[1] // tk
    hs = range(n_heads)
    tile = lambda h: slice((h // heads_per_tile) * LANES, (h // heads_per_tile + 1) * LANES)

    @pl.when(qi == 0)
    def _():
        chan = lax.broadcasted_iota(jnp.int32, (LANES, tk), 0)
        for j in range(nk):
            for h in hs:
                lo = (h % heads_per_tile) * head_dim
                kt = kt_ref[tile(h), j * tk:(j + 1) * tk] * (scale * LOG2E)
                kb_ref[j, h] = jnp.where((chan >= lo) & (chan < lo + head_dim), kt, 0.0).astype(BF16)
        tri_ref[...] = _tri(tk)

    acc_ref[...] = jnp.zeros_like(acc_ref)
    qs = [q_ref[:, tile(h)] for h in hs]
    bias2 = [bias_ref[layer, c * n_heads + h] * LOG2E for h in hs]

    def logits(j):
        return [_dot(qs[h], kb_ref[j, h]) + bias2[h] for h in hs]

    tri = tri_ref[...]
    jn = jnp.maximum(qi - 1, 0)
    row = lax.broadcasted_iota(jnp.int32, (tq, tk), 0)
    col = lax.broadcasted_iota(jnp.int32, (tq, tk), 1)
    valid = col < row
    z_diag = logits(qi)
    z_next, locs = [], []
    for h in hs:
        lsn = jnp.where(valid, _log2_1m_beta(z_diag[h]), 0.0)
        z_next.append(_dot(qs[h], kb_ref[jn, h]) + bias2[h])
        locs.append(_suffix_sums(lsn, tri))
    for h in hs:
        a = jnp.where(valid, jnp.exp2(z_diag[h] + locs[h]), 0.0)
        car_ref[h] = jnp.broadcast_to(locs[h][:, 0:1], (tq, LANES))
        a_ref[h] = a.astype(BF16)
        s_ref[h] = z_next[h]

    def body(it, j_prev):
        j = qi - 1 - it
        jn = jnp.maximum(j - 1, 0)
        koff = pl.multiple_of(j_prev * tk, tk)
        tri = tri_ref[...]
        n = len(hs)
        locs = [None] * n

        def apply_prev(h):
            acc_ref[h] += _dot(a_ref[h], v_ref[pl.ds(koff, tk), tile(h)])

        def finish(h):
            carry = car_ref[h]
            a = jnp.exp2(s_ref[h] + locs[h] + jnp.tile(carry, (1, tk // LANES)))
            car_ref[h] = carry + jnp.broadcast_to(locs[h][:, 0:1], carry.shape)
            a_ref[h] = a.astype(BF16)

        apply_prev(0)
        for h in hs:
            lsn = _log2_1m_beta(s_ref[h])
            if h + 1 < n:
                apply_prev(h + 1)
            locs[h] = _suffix_sums(lsn, tri)
        for h in hs:
            z_next = _dot(qs[h], kb_ref[jn, h]) + bias2[h]
            finish(h)
            s_ref[h] = z_next
        return j

    j_last = lax.fori_loop(0, qi, body, qi)
    koff = pl.multiple_of(j_last * tk, tk)
    outs = [_dot(a_ref[h], v_ref[pl.ds(koff, tk), tile(h)]) for h in hs]

    lane = lax.broadcasted_iota(jnp.int32, (tq, LANES), 1)
    totals = [acc_ref[h] + outs[h] for h in hs]
    for t in range(width // LANES):
        out = totals[t * heads_per_tile]
        for h in range(1, heads_per_tile):
            out = jnp.where(lane >= h * head_dim, totals[t * heads_per_tile + h], out)
        o_ref[:, t * LANES:(t + 1) * LANES] = out.astype(o_ref.dtype)


def _attn_prompt(q, kt, v, sb_bias, layer, *, head_dim):
    m, w = q.shape
    _, batch, _, seq_len = kt.shape
    tq = tk = 256
    width = 2 * LANES
    n_heads = width // head_dim
    nq = seq_len // tq
    grid = (batch, w // width, nq)
    qspec = pl.BlockSpec((tq, width), lambda b, c, i: (b * nq + i, c))
    return pl.pallas_call(
        functools.partial(_attn_prompt_kernel, layer=layer, tk=tk, head_dim=head_dim,
                          scale=head_dim ** -0.5),
        grid=grid,
        in_specs=[pl.BlockSpec(memory_space=pltpu.SMEM), qspec,
                  pl.BlockSpec((None, None, width, seq_len), lambda b, c, i: (layer, b, c, 0)),
                  pl.BlockSpec((seq_len, width), lambda b, c, i: (b, c))],
        out_specs=qspec,
        out_shape=jax.ShapeDtypeStruct((m, w), BF16),
        scratch_shapes=[pltpu.VMEM((seq_len // tk, n_heads, LANES, tk), BF16),
                        pltpu.VMEM((tk, tk), BF16),
                        pltpu.VMEM((n_heads, tq, LANES), F32),
                        pltpu.VMEM((n_heads, tq, LANES), F32),
                        pltpu.VMEM((n_heads, tq, tk), F32),
                        pltpu.VMEM((n_heads, tq, tk), BF16)],
        compiler_params=_params(3), name="attn_prompt")(sb_bias, q, kt, v)


def _attn_sample_kernel(pt_ref, bias_ref, q_ref, kn_ref, vn_ref, *refs, heads, head_dim, scale, group):
    del pt_ref
    ck_refs, cv_refs = refs[:group], refs[group:2 * group]
    o_ref, wt_ref, tri_ref, acc_ref, car_ref = refs[2 * group:]
    j = pl.program_id(1)
    t, w = q_ref.shape
    page = ck_refs[0].shape[1]
    rows = heads * t

    def process(ks, vs, new_rows):
        wt, tri, bias2 = wt_ref[...], tri_ref[...], bias_ref[...]
        qk = _dot_nt if new_rows else _dot
        zs = [qk(wt, k.astype(BF16)) * (scale * LOG2E) + bias2 for k in ks]
        lsns = [_log2_1m_beta(z) for z in zs]
        if new_rows:
            row = lax.broadcasted_iota(jnp.int32, (rows, page), 0)
            col = lax.broadcasted_iota(jnp.int32, (rows, page), 1)
            valid = col < row % t
            lsns = [jnp.where(valid, lsn, 0.0) for lsn in lsns]
        locs = [_suffix_sums(lsn, tri) for lsn in lsns]
        carry = car_ref[...]
        avs = []
        for z, loc in zip(zs, locs):
            avs.append(jnp.exp2(z + loc + carry))
            carry = carry + jnp.broadcast_to(loc[:, 0:1], carry.shape)
        car_ref[...] = carry
        if new_rows:
            avs = [jnp.where(valid, a, 0.0) for a in avs]
        a_all = jnp.concatenate([a.astype(BF16) for a in avs], axis=1)
        if new_rows:
            acc_ref[...] += _dot(a_all, jnp.concatenate([v.astype(BF16) for v in vs], axis=0))
        else:
            acc_ref[...] += _dot_nt(a_all, jnp.concatenate([v.astype(BF16) for v in vs], axis=1))

    @pl.when(j == 0)
    def _():
        qt = jnp.tile(q_ref[...], (heads, 1))
        rowh = lax.broadcasted_iota(jnp.int32, (rows, w), 0) // t
        colh = lax.broadcasted_iota(jnp.int32, (rows, w), 1) // head_dim
        wt_ref[...] = jnp.where(rowh == colh, qt, 0.0).astype(BF16)
        tri_ref[...] = _tri(page)
        acc_ref[...] = jnp.zeros_like(acc_ref)
        car_ref[...] = jnp.zeros_like(car_ref)
        pad = jnp.zeros((page - t, w), F32)
        process([jnp.concatenate([kn_ref[...], pad], axis=0)],
                [jnp.concatenate([vn_ref[...], pad], axis=0)], True)

    @pl.when(j > 0)
    def _():
        process([r[...] for r in ck_refs], [r[...] for r in cv_refs], False)

    @pl.when(j == pl.num_programs(1) - 1)
    def _():
        heads_per_tile = LANES // head_dim
        lane = lax.broadcasted_iota(jnp.int32, (t, LANES), 1)
        for p in range(w // LANES):
            cols = slice(p * LANES, (p + 1) * LANES)
            r0 = p * heads_per_tile * t
            out = acc_ref[r0:r0 + t, cols]
            for h in range(1, heads_per_tile):
                out = jnp.where(lane >= h * head_dim, acc_ref[r0 + h * t:r0 + (h + 1) * t, cols], out)
            o_ref[:, cols] = out


def _attn_sample(q, k_new, v_new, cache_kt, cache_vt, page_table, bias_rows, layer, *, seq_len, head_dim):
    m, w = q.shape
    dbsz, n_pages = page_table.shape
    page = cache_kt.shape[3]
    heads = w // head_dim
    rows = heads * seq_len
    group = 8
    assert page == LANES and seq_len == SUBLANES and n_pages % group == 0
    grid = (dbsz, n_pages // group + 1)
    rowspec = pl.BlockSpec((seq_len, w), lambda b, j, pt: (b, 0))

    def pagespec(g):
        return pl.BlockSpec(
            (None, None, w, page),
            lambda b, j, pt: (layer, pt[b, n_pages - 1 - ((jnp.maximum(j, 1) - 1) * group + g)], 0, 0))

    pagespecs = [pagespec(g) for g in range(group)]
    grid_spec = pltpu.PrefetchScalarGridSpec(
        num_scalar_prefetch=1, grid=grid,
        in_specs=[pl.BlockSpec((rows, LANES), lambda b, j, pt: (0, 0)),
                  rowspec, rowspec, rowspec] + pagespecs + pagespecs,
        out_specs=rowspec,
        scratch_shapes=[pltpu.VMEM((rows, w), BF16),
                        pltpu.VMEM((page, page), BF16),
                        pltpu.VMEM((rows, w), F32),
                        pltpu.VMEM((rows, LANES), F32)])
    return pl.pallas_call(
        functools.partial(_attn_sample_kernel, heads=heads, head_dim=head_dim, scale=head_dim ** -0.5,
                          group=group),
        grid_spec=grid_spec, out_shape=jax.ShapeDtypeStruct((m, w), F32),
        compiler_params=_params(2), name="attn_sample")(
            page_table, bias_rows, q, k_new, v_new, *([cache_kt] * group), *([cache_vt] * group))


def _out_proj_kernel(z_ref, o_ref, x_ref, p_ref, wout_ref, wpg_ref, wpe_ref, g_ref, b_ref,
                     x1b_ref, r_ref, *, alpha, row_chunk):
    chunks = [slice(r0, r0 + row_chunk) for r0 in range(0, x_ref.shape[0], row_chunk)]
    g, b = g_ref[...], b_ref[...]
    mixed = [jnp.concatenate([z_ref[rows, :].astype(BF16), o_ref[rows, :].astype(BF16)], axis=1)
             for rows in chunks]
    ys = [alpha * x_ref[rows, :] + _dot(mx, wout_ref[...]) for rows, mx in zip(chunks, mixed)]
    x1s = [_layer_norm(y, g, b) for y in ys]
    x1bs = [x1.astype(BF16) for x1 in x1s]
    logits = [_dot(x1b, wpg_ref[...]) for x1b in x1bs]
    pes = [_dot(p_ref[rows, :].astype(BF16), wpe_ref[...]) for rows in chunks]
    for rows, x1, x1b, lg, pe in zip(chunks, x1s, x1bs, logits, pes):
        x1b_ref[rows, :] = x1b
        r_ref[rows, :] = alpha * x1 + pe * (1.0 / (1.0 + jnp.exp(-lg)))


def _out_proj(z, o, x, p, w_out_b, w_pg_b, w_pe_b, ln_g, ln_b, layer, *, alpha):
    m, d = x.shape
    c = z.shape[1]
    pdim = p.shape[-1]
    tm = min(OUT_ROWS, m)
    rowspec = lambda width: pl.BlockSpec((tm, width), lambda i: (i, 0))
    const = lambda r, cc: pl.BlockSpec((None, r, cc), lambda i: (layer, 0, 0),
                                       pipeline_mode=pl.Buffered(1))
    return pl.pallas_call(
        functools.partial(_out_proj_kernel, alpha=alpha, row_chunk=min(OUT_ROW_CHUNK, tm)),
        grid=(m // tm,),
        in_specs=[rowspec(c), rowspec(c), rowspec(d),
                  pl.BlockSpec((None, tm, pdim), lambda i: (layer, i, 0)),
                  const(2 * c, d), const(d, d), const(pdim, d), const(1, d), const(1, d)],
        out_specs=(rowspec(d), rowspec(d)),
        out_shape=(jax.ShapeDtypeStruct((m, d), BF16), jax.ShapeDtypeStruct((m, d), F32)),
        compiler_params=_params(1), name="out_proj")(
            z, o, x, p, w_out_b, w_pg_b, w_pe_b, ln_g, ln_b)


def _ffn_kernel(*refs, seq_len, carry_mode, row_chunk):
    if carry_mode:
        (x1b_ref, r_ref, wa_ref, wg_ref, wconv_ref, bconv_ref, wd_ref, g_ref, b_ref,
         out_ref, tail_ref, carry_ref) = refs
    else:
        (x1b_ref, r_ref, wa_ref, wg_ref, wconv_ref, bconv_ref, wd_ref, g_ref, b_ref, e_ref,
         out_ref, tail_ref) = refs
    i = pl.program_id(0)
    f = pl.program_id(1)
    tm = x1b_ref.shape[0]

    @pl.when(f == 0)
    def _():
        out_ref[...] = r_ref[...]

    if carry_mode:
        @pl.when(i % (seq_len // tm) == 0)
        def _():
            carry_ref[f] = jnp.zeros(carry_ref.shape[1:], F32)

        prev8 = carry_ref[f]
    chunks = [slice(r0, r0 + row_chunk) for r0 in range(0, tm, row_chunk)]
    ups = [(_dot(x1b_ref[rows, :], wa_ref[...]), _dot(x1b_ref[rows, :], wg_ref[...])) for rows in chunks]
    hs = []
    for rows, (a, gate) in zip(chunks, ups):
        if carry_mode:
            a_c = _conv3_carry(a, prev8, wconv_ref[...])
            prev8 = a[row_chunk - SUBLANES:, :]
        else:
            a_c = _conv3_state(a, e_ref[rows, :], wconv_ref[...], seq_len)
            tail_ref[rows, :] = a
        hs.append((_gelu_tanh(a_c + bconv_ref[...]) * gate).astype(BF16))
    for rows, h in zip(chunks, hs):
        out_ref[rows, :] += _dot(h, wd_ref[...])
    if carry_mode:
        carry_ref[f] = prev8
        tail_ref[...] = prev8

    @pl.when(f == pl.num_programs(1) - 1)
    def _():
        out_ref[...] = _layer_norm(out_ref[...], g_ref[...], b_ref[...])


def _ffn(x1b, r, w_up_t, w_conv, b_conv, w_down_b, ln_g, ln_b, layer, *, seq_len, state=None):
    m, d = r.shape
    dff = w_conv.shape[-1]
    carry_mode = state is None
    tf = w_up_t.shape[-1]
    nf = dff // tf
    tm = min(FFN_ROWS, m)
    assert w_up_t.shape[1] == 2 * nf
    rowspec = pl.BlockSpec((tm, d), lambda i, f: (i, 0))
    vec = lambda rows, width: pl.BlockSpec((None, rows, width), lambda i, f: (layer, 0, 0))
    in_specs = [rowspec, rowspec,
                pl.BlockSpec((None, None, d, tf), lambda i, f: (layer, f, 0, 0)),
                pl.BlockSpec((None, None, d, tf), lambda i, f: (layer, nf + f, 0, 0)),
                pl.BlockSpec((None, w_conv.shape[1], tf), lambda i, f: (layer, 0, f)),
                pl.BlockSpec((None, 1, tf), lambda i, f: (layer, 0, f)),
                pl.BlockSpec((None, tf, d), lambda i, f: (layer, f, 0)),
                vec(1, d), vec(1, d)]
    args = [x1b, r, w_up_t, w_up_t, w_conv, b_conv, w_down_b, ln_g, ln_b]
    scratch = []
    if carry_mode:
        assert seq_len % tm == 0
        tail_shape = jax.ShapeDtypeStruct((m // tm, SUBLANES, dff), F32)
        tail_spec = pl.BlockSpec((None, SUBLANES, tf), lambda i, f: (i, 0, f))
        scratch = [pltpu.VMEM((nf, SUBLANES, tf), F32)]
    else:
        assert tm == m and seq_len == SUBLANES
        tile = pl.BlockSpec((tm, tf), lambda i, f: (i, f))
        in_specs.append(tile)
        args.append(state)
        tail_shape = jax.ShapeDtypeStruct((m, dff), F32)
        tail_spec = tile
    return pl.pallas_call(
        functools.partial(_ffn_kernel, seq_len=seq_len, carry_mode=carry_mode,
                          row_chunk=min(FFN_ROW_CHUNK, tm)),
        grid=(m // tm, nf), in_specs=in_specs, out_specs=(rowspec, tail_spec),
        out_shape=(jax.ShapeDtypeStruct((m, d), F32), tail_shape),
        scratch_shapes=scratch, compiler_params=_params(2), name="ffn")(*args)


def _state_rows(state, seq_len):
    b, k1, c = state.shape
    return jnp.pad(state, ((0, 0), (0, seq_len - k1), (0, 0))).reshape(b * seq_len, c)


def _last_rows(tails, batch, n):
    per_seq = tails.shape[0] // batch
    return tails[per_seq - 1::per_seq, SUBLANES - n:]


def kernel(x_prompt, x_sample, cache_k, cache_v, state_conv_mix, state_conv_ffn, page_table, p_prompt,
           p_sample, w_in, w_conv_mix, sb_bias, w_out, ln1_g, ln1_b, w_up, w_ffn_conv, b_ffn_conv,
           w_down, w_pg, w_pe, ln2_g, ln2_b):
    bsz, seq, d = x_prompt.shape
    dbsz, dseq, _ = x_sample.shape
    depth, n_pool, page, heads, head_dim = cache_k.shape
    width = heads * head_dim
    conv_k = w_conv_mix.shape[1]
    assert conv_k == 3 and w_ffn_conv.shape[1] == 3
    alpha = (2.0 * depth) ** 0.25

    w_in_b, w_out_b = w_in.astype(BF16), w_out.astype(BF16)
    w_up_t = w_up.reshape(depth, d, -1, FFN_COLS).transpose(0, 2, 1, 3).astype(BF16)
    w_down_b, w_pg_b, w_pe_b = w_down.astype(BF16), w_pg.astype(BF16), w_pe.astype(BF16)
    vec3 = lambda a: a.reshape(depth, 1, a.shape[-1])
    ln1_g3, ln1_b3, ln2_g3, ln2_b3, b_conv3 = map(vec3, (ln1_g, ln1_b, ln2_g, ln2_b, b_ffn_conv))
    cache_kt = cache_k.transpose(0, 1, 3, 4, 2).reshape(depth, n_pool, width, page)
    cache_vt = cache_v.transpose(0, 1, 3, 4, 2).reshape(depth, n_pool, width, page)
    pp = p_prompt.reshape(depth, bsz * seq, -1)
    ps = p_sample.reshape(depth, dbsz * dseq, -1)
    bias_rows = jnp.broadcast_to((sb_bias * LOG2E)[:, :, None, None], (depth, heads, dseq, LANES)
                                 ).reshape(depth, heads * dseq, LANES)

    xp = x_prompt.reshape(bsz * seq, d)
    xs = x_sample.reshape(dbsz * dseq, d)
    outs = [[] for _ in range(6)]
    kv = None
    for l in range(depth):
        shared_o = (w_out_b, w_pg_b, w_pe_b, ln1_g3, ln1_b3, l)
        shared_f = (w_up_t, w_ffn_conv, b_conv3, w_down_b, ln2_g3, ln2_b3, l)
        z, q, *kv, vb, mix_tail = _in_proj(xp, w_in_b, w_conv_mix, l, seq_len=seq, q_dtype=BF16,
                                           kv_buffers=kv)
        o = _attn_prompt(q, kv[0], vb, sb_bias, l, head_dim=head_dim)
        x1b, r = _out_proj(z, o, xp, pp, *shared_o, alpha=alpha)
        xp, ffn_tail = _ffn(x1b, r, *shared_f, seq_len=seq)
        outs[0].append(_last_rows(mix_tail, bsz, conv_k - 1))
        outs[1].append(_last_rows(ffn_tail, bsz, conv_k - 1))
        z, q, k, v, mix_tail = _in_proj(xs, w_in_b, w_conv_mix, l, seq_len=dseq, q_dtype=F32,
                                        state=_state_rows(state_conv_mix[l], dseq))
        o = _attn_sample(q, k, v, cache_kt, cache_vt, page_table, bias_rows[l], l,
                         seq_len=dseq, head_dim=head_dim)
        x1b, r = _out_proj(z, o, xs, ps, *shared_o, alpha=alpha)
        xs, ffn_tail = _ffn(x1b, r, *shared_f, seq_len=dseq, state=_state_rows(state_conv_ffn[l], dseq))
        outs[2].append(k.reshape(dbsz, dseq, heads, head_dim))
        outs[3].append(v.reshape(dbsz, dseq, heads, head_dim))
        outs[4].append(mix_tail.reshape(dbsz, dseq, -1)[:, dseq - (conv_k - 1):])
        outs[5].append(ffn_tail.reshape(dbsz, dseq, -1)[:, dseq - (conv_k - 1):])
    kv_prompt = [a.reshape(depth, bsz, heads, head_dim, seq).transpose(0, 1, 4, 2, 3) for a in kv]
    stacked = [jnp.stack(o) for o in outs]
    return (xp.reshape(bsz, seq, d), xs.reshape(dbsz, dseq, d), *kv_prompt, *stacked)
```

```python
import functools

import jax
import jax.numpy as jnp
from jax import lax
from jax.experimental import pallas as pl
from jax.experimental.pallas import tpu as pltpu

BF16 = jnp.bfloat16
F32 = jnp.float32

LN_EPS = 1e-5
LANES = 128
SUBLANES = 8
VMEM_LIMIT = 56 * 1024 * 1024
OUT_ROWS = 512
OUT_ROW_CHUNK = 256
FFN_COLS = 256
FFN_ROWS = 1024
FFN_ROW_CHUNK = 512

_dot = functools.partial(jnp.dot, preferred_element_type=F32)


def _dot_nt(a, b):
    return lax.dot_general(a, b, (((1,), (1,)), ((), ())), preferred_element_type=F32)


def _params(n_grid):
    return pltpu.CompilerParams(dimension_semantics=("arbitrary",) * n_grid,
                                vmem_limit_bytes=VMEM_LIMIT)


def _shift_rows_carry(u, prev8, shift):
    p = pltpu.roll(u, shift, axis=0)
    row8 = lax.broadcasted_iota(jnp.int32, prev8.shape, 0)
    top = jnp.where(row8 < shift, pltpu.roll(prev8, shift, axis=0), p[0:SUBLANES])
    return jnp.concatenate([top, p[SUBLANES:]], axis=0)


def _conv3_carry(u, prev8, w):
    return (w[0:1] * _shift_rows_carry(u, prev8, 2) + w[1:2] * _shift_rows_carry(u, prev8, 1)
            + w[2:3] * u)


def _conv3_state(u, e, w, seq_len):
    tm = u.shape[0]
    pos = lax.broadcasted_iota(jnp.int32, u.shape, 0) % seq_len
    p1 = jnp.where(pos >= 1, pltpu.roll(u, 1, axis=0), pltpu.roll(e, tm - 1, axis=0))
    p2 = jnp.where(pos >= 2, pltpu.roll(u, 2, axis=0), e)
    return w[0:1] * p2 + w[1:2] * p1 + w[2:3] * u


def _layer_norm(y, g, b):
    mu = jnp.mean(y, axis=-1, keepdims=True)
    d = y - mu
    var = jnp.mean(d * d, axis=-1, keepdims=True)
    return d * lax.rsqrt(var + LN_EPS) * g + b


def _gelu_tanh(x):
    return x * (0.5 * (1.0 + jnp.tanh(0.7978845608028654 * (x + 0.044715 * (x * x * x)))))


LOG2E = 1.4426950408889634


def _log2_1m_beta(z2):
    nz = -z2
    return jnp.minimum(nz, 0.0) - jnp.log(1.0 + jnp.exp2(jnp.minimum(z2, nz))) * LOG2E


def _suffix_sums(lsn, tri):
    return _dot(lsn.astype(BF16), tri)


def _tri(n):
    r = lax.broadcasted_iota(jnp.int32, (n, n), 0)
    c = lax.broadcasted_iota(jnp.int32, (n, n), 1)
    return jnp.where(r >= c, 1.0, 0.0).astype(BF16)


def _in_proj_kernel(*refs, seq_len, carry_mode, n_aliased):
    if carry_mode:
        refs = refs[:8] + refs[8 + n_aliased:]
        (x_ref, wb_ref, wc_ref, wx_ref, wq_ref, wk_ref, wv_ref, wconv_ref,
         z_ref, q_ref, kt_ref, vt_ref, vb_ref, tail_ref, carry_ref) = refs
    else:
        (x_ref, wb_ref, wc_ref, wx_ref, wq_ref, wk_ref, wv_ref, wconv_ref, e_ref,
         z_ref, q_ref, k_ref, v_ref, tail_ref) = refs
    tm = x_ref.shape[0]
    xb = x_ref[...].astype(BF16)
    u = _dot(xb, wc_ref[...]) * _dot(xb, wx_ref[...])
    if carry_mode:
        i = pl.program_id(1)

        @pl.when(i % (seq_len // tm) == 0)
        def _():
            carry_ref[...] = jnp.zeros_like(carry_ref)

        y = _conv3_carry(u, carry_ref[...], wconv_ref[...])
        carry_ref[...] = u[tm - SUBLANES:, :]
        tail_ref[...] = u[tm - SUBLANES:, :]
    else:
        y = _conv3_state(u, e_ref[...], wconv_ref[...], seq_len)
        tail_ref[...] = u
    z_ref[...] = (_dot(xb, wb_ref[...]) * y).astype(z_ref.dtype)
    q_ref[...] = _dot(xb, wq_ref[...]).astype(q_ref.dtype)
    k = _dot(xb, wk_ref[...])
    v = _dot(xb, wv_ref[...])
    if carry_mode:
        kt_ref[...] = k.T
        vt_ref[...] = v.T
        vb_ref[...] = v.astype(BF16)
    else:
        k_ref[...] = k
        v_ref[...] = v


def _in_proj(x, w_in_b, w_conv, layer, *, seq_len, q_dtype, state=None, kv_buffers=None):
    m, d = x.shape
    c = w_conv.shape[-1]
    carry_mode = state is None
    tm = min(512, m)
    tn = 512
    nb = c // tn
    batch = m // seq_len
    grid = (nb, m // tm)

    def wspec(s):
        return pl.BlockSpec((None, d, tn), lambda j, i, s=s: (layer, 0, s * nb + j))

    in_specs = [pl.BlockSpec((tm, d), lambda j, i: (i, 0))]
    in_specs += [wspec(s) for s in range(6)]
    in_specs += [pl.BlockSpec((None, w_conv.shape[1], tn), lambda j, i: (layer, 0, j))]
    args = [x] + [w_in_b] * 6 + [w_conv]
    tile = pl.BlockSpec((tm, tn), lambda j, i: (i, j))
    rows = lambda dt: jax.ShapeDtypeStruct((m, c), dt)
    scratch = []
    aliases = {}
    if carry_mode:
        assert seq_len % tm == 0
        per_seq = seq_len // tm
        tspec = pl.BlockSpec((None, None, tn, tm), lambda j, i: (layer, i // per_seq, j, i % per_seq))
        tshape = jax.ShapeDtypeStruct((w_in_b.shape[0], batch, c, seq_len), F32)
        if kv_buffers is not None:
            aliases = {len(args): 2, len(args) + 1: 3}
            in_specs += [pl.BlockSpec(memory_space=pl.ANY)] * 2
            args += list(kv_buffers)
        tail_shape = jax.ShapeDtypeStruct((m // tm, SUBLANES, c), F32)
        tail_spec = pl.BlockSpec((None, SUBLANES, tn), lambda j, i: (i, 0, j))
        scratch = [pltpu.VMEM((SUBLANES, tn), F32)]
        out_shape = (rows(BF16), rows(q_dtype), tshape, tshape, rows(BF16), tail_shape)
        out_specs = (tile, tile, tspec, tspec, tile, tail_spec)
    else:
        assert tm == m and seq_len == SUBLANES
        in_specs.append(tile)
        args.append(state)
        out_shape = (rows(BF16), rows(q_dtype), rows(F32), rows(F32), rows(F32))
        out_specs = (tile,) * 5
    return pl.pallas_call(
        functools.partial(_in_proj_kernel, seq_len=seq_len, carry_mode=carry_mode, n_aliased=len(aliases)),
        grid=grid, in_specs=in_specs, out_specs=out_specs, input_output_aliases=aliases,
        out_shape=out_shape, scratch_shapes=scratch, compiler_params=_params(2),
        name="in_proj")(*args)


def _attn_prompt_kernel(bias_ref, q_ref, kt_ref, v_ref, o_ref,
                        kb_ref, tri_ref, acc_ref, car_ref, s_ref, a_ref, *, layer, tk, head_dim, scale):
    c = pl.program_id(1)
    qi = pl.program_id(2)
    tq, width = q_ref.shape
    heads_per_tile = LANES // head_dim
    n_heads = width // head_dim
    nk = kt_ref.shape[1] // tk
    hs = range(n_heads)
    tile = lambda h: slice((h // heads_per_tile) * LANES, (h // heads_per_tile + 1) * LANES)

    @pl.when(qi == 0)
    def _():
        chan = lax.broadcasted_iota(jnp.int32, (LANES, tk), 0)
        for j in range(nk):
            for h in hs:
                lo = (h % heads_per_tile) * head_dim
                kt = kt_ref[tile(h), j * tk:(j + 1) * tk] * (scale * LOG2E)
                kb_ref[j, h] = jnp.where((chan >= lo) & (chan < lo + head_dim), kt, 0.0).astype(BF16)
        tri_ref[...] = _tri(tk)

    acc_ref[...] = jnp.zeros_like(acc_ref)
    qs = [q_ref[:, tile(h)] for h in hs]
    bias2 = [bias_ref[layer, c * n_heads + h] * LOG2E for h in hs]

    def logits(j):
        return [_dot(qs[h], kb_ref[j, h]) + bias2[h] for h in hs]

    tri = tri_ref[...]
    jn = jnp.maximum(qi - 1, 0)
    row = lax.broadcasted_iota(jnp.int32, (tq, tk), 0)
    col = lax.broadcasted_iota(jnp.int32, (tq, tk), 1)
    valid = col < row
    z_diag = logits(qi)
    z_next, locs = [], []
    for h in hs:
        lsn = jnp.where(valid, _log2_1m_beta(z_diag[h]), 0.0)
        z_next.append(_dot(qs[h], kb_ref[jn, h]) + bias2[h])
        locs.append(_suffix_sums(lsn, tri))
    for h in hs:
        a = jnp.where(valid, jnp.exp2(z_diag[h] + locs[h]), 0.0)
        car_ref[h] = jnp.broadcast_to(locs[h][:, 0:1], (tq, LANES))
        a_ref[h] = a.astype(BF16)
        s_ref[h] = z_next[h]

    def body(it, j_prev):
        j = qi - 1 - it
        jn = jnp.maximum(j - 1, 0)
        koff = pl.multiple_of(j_prev * tk, tk)
        tri = tri_ref[...]
        n = len(hs)
        locs = [None] * n

        def apply_prev(h):
            acc_ref[h] += _dot(a_ref[h], v_ref[pl.ds(koff, tk), tile(h)])

        def finish(h):
            carry = car_ref[h]
            a = jnp.exp2(s_ref[h] + locs[h] + jnp.tile(carry, (1, tk // LANES)))
            car_ref[h] = carry + jnp.broadcast_to(locs[h][:, 0:1], carry.shape)
            a_ref[h] = a.astype(BF16)

        apply_prev(0)
        for h in hs:
            lsn = _log2_1m_beta(s_ref[h])
            if h + 1 < n:
                apply_prev(h + 1)
            locs[h] = _suffix_sums(lsn, tri)
        for h in hs:
            z_next = _dot(qs[h], kb_ref[jn, h]) + bias2[h]
            finish(h)
            s_ref[h] = z_next
        return j

    j_last = lax.fori_loop(0, qi, body, qi)
    koff = pl.multiple_of(j_last * tk, tk)
    outs = [_dot(a_ref[h], v_ref[pl.ds(koff, tk), tile(h)]) for h in hs]

    lane = lax.broadcasted_iota(jnp.int32, (tq, LANES), 1)
    totals = [acc_ref[h] + outs[h] for h in hs]
    for t in range(width // LANES):
        out = totals[t * heads_per_tile]
        for h in range(1, heads_per_tile):
            out = jnp.where(lane >= h * head_dim, totals[t * heads_per_tile + h], out)
        o_ref[:, t * LANES:(t + 1) * LANES] = out.astype(o_ref.dtype)


def _attn_prompt(q, kt, v, sb_bias, layer, *, head_dim):
    m, w = q.shape
    _, batch, _, seq_len = kt.shape
    tq = tk = 256
    width = 4 * LANES
    n_heads = width // head_dim
    nq = seq_len // tq
    grid = (batch, w // width, nq)
    qspec = pl.BlockSpec((tq, width), lambda b, c, i: (b * nq + i, c))
    return pl.pallas_call(
        functools.partial(_attn_prompt_kernel, layer=layer, tk=tk, head_dim=head_dim,
                          scale=head_dim ** -0.5),
        grid=grid,
        in_specs=[pl.BlockSpec(memory_space=pltpu.SMEM), qspec,
                  pl.BlockSpec((None, None, width, seq_len), lambda b, c, i: (layer, b, c, 0)),
                  pl.BlockSpec((seq_len, width), lambda b, c, i: (b, c))],
        out_specs=qspec,
        out_shape=jax.ShapeDtypeStruct((m, w), BF16),
        scratch_shapes=[pltpu.VMEM((seq_len // tk, n_heads, LANES, tk), BF16),
                        pltpu.VMEM((tk, tk), BF16),
                        pltpu.VMEM((n_heads, tq, LANES), F32),
                        pltpu.VMEM((n_heads, tq, LANES), F32),
                        pltpu.VMEM((n_heads, tq, tk), F32),
                        pltpu.VMEM((n_heads, tq, tk), BF16)],
        compiler_params=_params(3), name="attn_prompt")(sb_bias, q, kt, v)


def _attn_sample_kernel(pt_ref, bias_ref, q_ref, kn_ref, vn_ref, *refs, heads, head_dim, scale, group):
    del pt_ref
    ck_refs, cv_refs = refs[:group], refs[group:2 * group]
    o_ref, wt_ref, tri_ref, acc_ref, car_ref = refs[2 * group:]
    j = pl.program_id(1)
    t, w = q_ref.shape
    page = ck_refs[0].shape[1]
    rows = heads * t

    def process(ks, vs, new_rows):
        wt, tri, bias2 = wt_ref[...], tri_ref[...], bias_ref[...]
        qk = _dot_nt if new_rows else _dot
        zs = [qk(wt, k.astype(BF16)) * (scale * LOG2E) + bias2 for k in ks]
        lsns = [_log2_1m_beta(z) for z in zs]
        if new_rows:
            row = lax.broadcasted_iota(jnp.int32, (rows, page), 0)
            col = lax.broadcasted_iota(jnp.int32, (rows, page), 1)
            valid = col < row % t
            lsns = [jnp.where(valid, lsn, 0.0) for lsn in lsns]
        locs = [_suffix_sums(lsn, tri) for lsn in lsns]
        carry = car_ref[...]
        avs = []
        for z, loc in zip(zs, locs):
            avs.append(jnp.exp2(z + loc + carry))
            carry = carry + jnp.broadcast_to(loc[:, 0:1], carry.shape)
        car_ref[...] = carry
        if new_rows:
            avs = [jnp.where(valid, a, 0.0) for a in avs]
        a_all = jnp.concatenate([a.astype(BF16) for a in avs], axis=1)
        if new_rows:
            acc_ref[...] += _dot(a_all, jnp.concatenate([v.astype(BF16) for v in vs], axis=0))
        else:
            acc_ref[...] += _dot_nt(a_all, jnp.concatenate([v.astype(BF16) for v in vs], axis=1))

    @pl.when(j == 0)
    def _():
        qt = jnp.tile(q_ref[...], (heads, 1))
        rowh = lax.broadcasted_iota(jnp.int32, (rows, w), 0) // t
        colh = lax.broadcasted_iota(jnp.int32, (rows, w), 1) // head_dim
        wt_ref[...] = jnp.where(rowh == colh, qt, 0.0).astype(BF16)
        tri_ref[...] = _tri(page)
        acc_ref[...] = jnp.zeros_like(acc_ref)
        car_ref[...] = jnp.zeros_like(car_ref)
        pad = jnp.zeros((page - t, w), F32)
        process([jnp.concatenate([kn_ref[...], pad], axis=0)],
                [jnp.concatenate([vn_ref[...], pad], axis=0)], True)

    @pl.when(j > 0)
    def _():
        process([r[...] for r in ck_refs], [r[...] for r in cv_refs], False)

    @pl.when(j == pl.num_programs(1) - 1)
    def _():
        heads_per_tile = LANES // head_dim
        lane = lax.broadcasted_iota(jnp.int32, (t, LANES), 1)
        for p in range(w // LANES):
            cols = slice(p * LANES, (p + 1) * LANES)
            r0 = p * heads_per_tile * t
            out = acc_ref[r0:r0 + t, cols]
            for h in range(1, heads_per_tile):
                out = jnp.where(lane >= h * head_dim, acc_ref[r0 + h * t:r0 + (h + 1) * t, cols], out)
            o_ref[:, cols] = out


def _attn_sample(q, k_new, v_new, cache_kt, cache_vt, page_table, bias_rows, layer, *, seq_len, head_dim):
    m, w = q.shape
    dbsz, n_pages = page_table.shape
    page = cache_kt.shape[3]
    heads = w // head_dim
    rows = heads * seq_len
    group = 16
    assert page == LANES and seq_len == SUBLANES and n_pages % group == 0
    grid = (dbsz, n_pages // group + 1)
    rowspec = pl.BlockSpec((seq_len, w), lambda b, j, pt: (b, 0))

    def pagespec(g):
        return pl.BlockSpec(
            (None, None, w, page),
            lambda b, j, pt: (layer, pt[b, n_pages - 1 - ((jnp.maximum(j, 1) - 1) * group + g)], 0, 0))

    pagespecs = [pagespec(g) for g in range(group)]
    grid_spec = pltpu.PrefetchScalarGridSpec(
        num_scalar_prefetch=1, grid=grid,
        in_specs=[pl.BlockSpec((rows, LANES), lambda b, j, pt: (0, 0)),
                  rowspec, rowspec, rowspec] + pagespecs + pagespecs,
        out_specs=rowspec,
        scratch_shapes=[pltpu.VMEM((rows, w), BF16),
                        pltpu.VMEM((page, page), BF16),
                        pltpu.VMEM((rows, w), F32),
                        pltpu.VMEM((rows, LANES), F32)])
    return pl.pallas_call(
        functools.partial(_attn_sample_kernel, heads=heads, head_dim=head_dim, scale=head_dim ** -0.5,
                          group=group),
        grid_spec=grid_spec, out_shape=jax.ShapeDtypeStruct((m, w), F32),
        compiler_params=_params(2), name="attn_sample")(
            page_table, bias_rows, q, k_new, v_new, *([cache_kt] * group), *([cache_vt] * group))


def _out_proj_kernel(z_ref, o_ref, x_ref, p_ref, wout_ref, wpg_ref, wpe_ref, g_ref, b_ref,
                     x1b_ref, r_ref, *, alpha, row_chunk):
    chunks = [slice(r0, r0 + row_chunk) for r0 in range(0, x_ref.shape[0], row_chunk)]
    g, b = g_ref[...], b_ref[...]
    mixed = [jnp.concatenate([z_ref[rows, :].astype(BF16), o_ref[rows, :].astype(BF16)], axis=1)
             for rows in chunks]
    ys = [alpha * x_ref[rows, :] + _dot(mx, wout_ref[...]) for rows, mx in zip(chunks, mixed)]
    x1s = [_layer_norm(y, g, b) for y in ys]
    x1bs = [x1.astype(BF16) for x1 in x1s]
    logits = [_dot(x1b, wpg_ref[...]) for x1b in x1bs]
    pes = [_dot(p_ref[rows, :].astype(BF16), wpe_ref[...]) for rows in chunks]
    for rows, x1, x1b, lg, pe in zip(chunks, x1s, x1bs, logits, pes):
        x1b_ref[rows, :] = x1b
        r_ref[rows, :] = alpha * x1 + pe * (1.0 / (1.0 + jnp.exp(-lg)))


def _out_proj(z, o, x, p, w_out_b, w_pg_b, w_pe_b, ln_g, ln_b, layer, *, alpha):
    m, d = x.shape
    c = z.shape[1]
    pdim = p.shape[-1]
    tm = min(OUT_ROWS, m)
    rowspec = lambda width: pl.BlockSpec((tm, width), lambda i: (i, 0))
    const = lambda r, cc: pl.BlockSpec((None, r, cc), lambda i: (layer, 0, 0),
                                       pipeline_mode=pl.Buffered(1))
    return pl.pallas_call(
        functools.partial(_out_proj_kernel, alpha=alpha, row_chunk=min(OUT_ROW_CHUNK, tm)),
        grid=(m // tm,),
        in_specs=[rowspec(c), rowspec(c), rowspec(d),
                  pl.BlockSpec((None, tm, pdim), lambda i: (layer, i, 0)),
                  const(2 * c, d), const(d, d), const(pdim, d), const(1, d), const(1, d)],
        out_specs=(rowspec(d), rowspec(d)),
        out_shape=(jax.ShapeDtypeStruct((m, d), BF16), jax.ShapeDtypeStruct((m, d), F32)),
        compiler_params=_params(1), name="out_proj")(
            z, o, x, p, w_out_b, w_pg_b, w_pe_b, ln_g, ln_b)


def _ffn_kernel(*refs, seq_len, carry_mode, row_chunk):
    if carry_mode:
        (x1b_ref, r_ref, wa_ref, wg_ref, wconv_ref, bconv_ref, wd_ref, g_ref, b_ref,
         out_ref, tail_ref, carry_ref) = refs
    else:
        (x1b_ref, r_ref, wa_ref, wg_ref, wconv_ref, bconv_ref, wd_ref, g_ref, b_ref, e_ref,
         out_ref, tail_ref) = refs
    i = pl.program_id(0)
    f = pl.program_id(1)
    tm = x1b_ref.shape[0]

    @pl.when(f == 0)
    def _():
        out_ref[...] = r_ref[...]

    if carry_mode:
        @pl.when(i % (seq_len // tm) == 0)
        def _():
            carry_ref[f] = jnp.zeros(carry_ref.shape[1:], F32)

        prev8 = carry_ref[f]
    chunks = [slice(r0, r0 + row_chunk) for r0 in range(0, tm, row_chunk)]
    ups = [(_dot(x1b_ref[rows, :], wa_ref[...]), _dot(x1b_ref[rows, :], wg_ref[...])) for rows in chunks]
    hs = []
    for rows, (a, gate) in zip(chunks, ups):
        if carry_mode:
            a_c = _conv3_carry(a, prev8, wconv_ref[...])
            prev8 = a[row_chunk - SUBLANES:, :]
        else:
            a_c = _conv3_state(a, e_ref[rows, :], wconv_ref[...], seq_len)
            tail_ref[rows, :] = a
        hs.append((_gelu_tanh(a_c + bconv_ref[...]) * gate).astype(BF16))
    for rows, h in zip(chunks, hs):
        out_ref[rows, :] += _dot(h, wd_ref[...])
    if carry_mode:
        carry_ref[f] = prev8
        tail_ref[...] = prev8

    @pl.when(f == pl.num_programs(1) - 1)
    def _():
        out_ref[...] = _layer_norm(out_ref[...], g_ref[...], b_ref[...])


def _ffn(x1b, r, w_up_t, w_conv, b_conv, w_down_b, ln_g, ln_b, layer, *, seq_len, state=None):
    m, d = r.shape
    dff = w_conv.shape[-1]
    carry_mode = state is None
    tf = w_up_t.shape[-1]
    nf = dff // tf
    tm = min(FFN_ROWS, m)
    assert w_up_t.shape[1] == 2 * nf
    rowspec = pl.BlockSpec((tm, d), lambda i, f: (i, 0))
    vec = lambda rows, width: pl.BlockSpec((None, rows, width), lambda i, f: (layer, 0, 0))
    in_specs = [rowspec, rowspec,
                pl.BlockSpec((None, None, d, tf), lambda i, f: (layer, f, 0, 0)),
                pl.BlockSpec((None, None, d, tf), lambda i, f: (layer, nf + f, 0, 0)),
                pl.BlockSpec((None, w_conv.shape[1], tf), lambda i, f: (layer, 0, f)),
                pl.BlockSpec((None, 1, tf), lambda i, f: (layer, 0, f)),
                pl.BlockSpec((None, tf, d), lambda i, f: (layer, f, 0)),
                vec(1, d), vec(1, d)]
    args = [x1b, r, w_up_t, w_up_t, w_conv, b_conv, w_down_b, ln_g, ln_b]
    scratch = []
    if carry_mode:
        assert seq_len % tm == 0
        tail_shape = jax.ShapeDtypeStruct((m // tm, SUBLANES, dff), F32)
        tail_spec = pl.BlockSpec((None, SUBLANES, tf), lambda i, f: (i, 0, f))
        scratch = [pltpu.VMEM((nf, SUBLANES, tf), F32)]
    else:
        assert tm == m and seq_len == SUBLANES
        tile = pl.BlockSpec((tm, tf), lambda i, f: (i, f))
        in_specs.append(tile)
        args.append(state)
        tail_shape = jax.ShapeDtypeStruct((m, dff), F32)
        tail_spec = tile
    return pl.pallas_call(
        functools.partial(_ffn_kernel, seq_len=seq_len, carry_mode=carry_mode,
                          row_chunk=min(FFN_ROW_CHUNK, tm)),
        grid=(m // tm, nf), in_specs=in_specs, out_specs=(rowspec, tail_spec),
        out_shape=(jax.ShapeDtypeStruct((m, d), F32), tail_shape),
        scratch_shapes=scratch, compiler_params=_params(2), name="ffn")(*args)


def _state_rows(state, seq_len):
    b, k1, c = state.shape
    return jnp.pad(state, ((0, 0), (0, seq_len - k1), (0, 0))).reshape(b * seq_len, c)


def _last_rows(tails, batch, n):
    per_seq = tails.shape[0] // batch
    return tails[per_seq - 1::per_seq, SUBLANES - n:]


def kernel(x_prompt, x_sample, cache_k, cache_v, state_conv_mix, state_conv_ffn, page_table, p_prompt,
           p_sample, w_in, w_conv_mix, sb_bias, w_out, ln1_g, ln1_b, w_up, w_ffn_conv, b_ffn_conv,
           w_down, w_pg, w_pe, ln2_g, ln2_b):
    bsz, seq, d = x_prompt.shape
    dbsz, dseq, _ = x_sample.shape
    depth, n_pool, page, heads, head_dim = cache_k.shape
    width = heads * head_dim
    conv_k = w_conv_mix.shape[1]
    assert conv_k == 3 and w_ffn_conv.shape[1] == 3
    alpha = (2.0 * depth) ** 0.25

    w_in_b, w_out_b = w_in.astype(BF16), w_out.astype(BF16)
    w_up_t = w_up.reshape(depth, d, -1, FFN_COLS).transpose(0, 2, 1, 3).astype(BF16)
    w_down_b, w_pg_b, w_pe_b = w_down.astype(BF16), w_pg.astype(BF16), w_pe.astype(BF16)
    vec3 = lambda a: a.reshape(depth, 1, a.shape[-1])
    ln1_g3, ln1_b3, ln2_g3, ln2_b3, b_conv3 = map(vec3, (ln1_g, ln1_b, ln2_g, ln2_b, b_ffn_conv))
    cache_kt = cache_k.transpose(0, 1, 3, 4, 2).reshape(depth, n_pool, width, page)
    cache_vt = cache_v.transpose(0, 1, 3, 4, 2).reshape(depth, n_pool, width, page)
    pp = p_prompt.reshape(depth, bsz * seq, -1)
    ps = p_sample.reshape(depth, dbsz * dseq, -1)
    bias_rows = jnp.broadcast_to((sb_bias * LOG2E)[:, :, None, None], (depth, heads, dseq, LANES)
                                 ).reshape(depth, heads * dseq, LANES)

    xp = x_prompt.reshape(bsz * seq, d)
    xs = x_sample.reshape(dbsz * dseq, d)
    outs = [[] for _ in range(6)]
    kv = None
    for l in range(depth):
        shared_o = (w_out_b, w_pg_b, w_pe_b, ln1_g3, ln1_b3, l)
        shared_f = (w_up_t, w_ffn_conv, b_conv3, w_down_b, ln2_g3, ln2_b3, l)
        z, q, *kv, vb, mix_tail = _in_proj(xp, w_in_b, w_conv_mix, l, seq_len=seq, q_dtype=BF16,
                                           kv_buffers=kv)
        o = _attn_prompt(q, kv[0], vb, sb_bias, l, head_dim=head_dim)
        x1b, r = _out_proj(z, o, xp, pp, *shared_o, alpha=alpha)
        xp, ffn_tail = _ffn(x1b, r, *shared_f, seq_len=seq)
        outs[0].append(_last_rows(mix_tail, bsz, conv_k - 1))
        outs[1].append(_last_rows(ffn_tail, bsz, conv_k - 1))
        z, q, k, v, mix_tail = _in_proj(xs, w_in_b, w_conv_mix, l, seq_len=dseq, q_dtype=F32,
                                        state=_state_rows(state_conv_mix[l], dseq))
        o = _attn_sample(q, k, v, cache_kt, cache_vt, page_table, bias_rows[l], l,
                         seq_len=dseq, head_dim=head_dim)
        x1b, r = _out_proj(z, o, xs, ps, *shared_o, alpha=alpha)
        xs, ffn_tail = _ffn(x1b, r, *shared_f, seq_len=dseq, state=_state_rows(state_conv_ffn[l], dseq))
        outs[2].append(k.reshape(dbsz, dseq, heads, head_dim))
        outs[3].append(v.reshape(dbsz, dseq, heads, head_dim))
        outs[4].append(mix_tail.reshape(dbsz, dseq, -1)[:, dseq - (conv_k - 1):])
        outs[5].append(ffn_tail.reshape(dbsz, dseq, -1)[:, dseq - (conv_k - 1):])
    kv_prompt = [a.reshape(depth, bsz, heads, head_dim, seq).transpose(0, 1, 4, 2, 3) for a in kv]
    stacked = [jnp.stack(o) for o in outs]
    return (xp.reshape(bsz, seq, d), xs.reshape(dbsz, dseq, d), *kv_prompt, *stacked)
```

```python
import functools

import jax
import jax.numpy as jnp
from jax import lax
from jax.experimental import pallas as pl
from jax.experimental.pallas import tpu as pltpu

BF16 = jnp.bfloat16
F32 = jnp.float32

LN_EPS = 1e-5
LANES = 128
SUBLANES = 8
VMEM_LIMIT = 56 * 1024 * 1024

IN_ROWS = 512
IN_COLS = 512
ATTN_BLOCK = 256
ATTN_WIDTH = 8 * LANES
PAGE_GROUP = 16
OUT_ROWS = 512
OUT_ROW_CHUNK = 256
FFN_COLS = 256
FFN_ROWS = 1024
FFN_ROW_CHUNK = 512

_dot = functools.partial(jnp.dot, preferred_element_type=F32)


def _dot_nt(a, b):
    return lax.dot_general(a, b, (((1,), (1,)), ((), ())), preferred_element_type=F32)


def _params(n_grid):
    return pltpu.CompilerParams(dimension_semantics=("arbitrary",) * n_grid,
                                vmem_limit_bytes=VMEM_LIMIT)


def _shift_rows_carry(u, prev8, shift):
    p = pltpu.roll(u, shift, axis=0)
    row8 = lax.broadcasted_iota(jnp.int32, prev8.shape, 0)
    top = jnp.where(row8 < shift, pltpu.roll(prev8, shift, axis=0), p[0:SUBLANES])
    return jnp.concatenate([top, p[SUBLANES:]], axis=0)


def _conv3_carry(u, prev8, w):
    return (w[0:1] * _shift_rows_carry(u, prev8, 2) + w[1:2] * _shift_rows_carry(u, prev8, 1)
            + w[2:3] * u)


def _conv3_state(u, e, w, seq_len):
    tm = u.shape[0]
    pos = lax.broadcasted_iota(jnp.int32, u.shape, 0) % seq_len
    p1 = jnp.where(pos >= 1, pltpu.roll(u, 1, axis=0), pltpu.roll(e, tm - 1, axis=0))
    p2 = jnp.where(pos >= 2, pltpu.roll(u, 2, axis=0), e)
    return w[0:1] * p2 + w[1:2] * p1 + w[2:3] * u


def _layer_norm(y, g, b):
    mu = jnp.mean(y, axis=-1, keepdims=True)
    d = y - mu
    var = jnp.mean(d * d, axis=-1, keepdims=True)
    return d * lax.rsqrt(var + LN_EPS) * g + b


def _gelu_tanh(x):
    return x * (0.5 * (1.0 + jnp.tanh(0.7978845608028654 * (x + 0.044715 * (x * x * x)))))


LOG2E = 1.4426950408889634


def _log2_1m_beta(z2):
    nz = -z2
    return jnp.minimum(nz, 0.0) - jnp.log(1.0 + jnp.exp2(jnp.minimum(z2, nz))) * LOG2E


def _suffix_sums(lsn, tri):
    return _dot(lsn.astype(BF16), tri)


def _tri(n):
    r = lax.broadcasted_iota(jnp.int32, (n, n), 0)
    c = lax.broadcasted_iota(jnp.int32, (n, n), 1)
    return jnp.where(r >= c, 1.0, 0.0).astype(BF16)


def _in_proj_kernel(*refs, seq_len, carry_mode, n_aliased):
    if carry_mode:
        refs = refs[:8] + refs[8 + n_aliased:]
        (x_ref, wb_ref, wc_ref, wx_ref, wq_ref, wk_ref, wv_ref, wconv_ref,
         z_ref, q_ref, kt_ref, vt_ref, vb_ref, tail_ref, carry_ref) = refs
    else:
        (x_ref, wb_ref, wc_ref, wx_ref, wq_ref, wk_ref, wv_ref, wconv_ref, e_ref,
         z_ref, q_ref, k_ref, v_ref, tail_ref) = refs
    tm = x_ref.shape[0]
    xb = x_ref[...].astype(BF16)
    u = _dot(xb, wc_ref[...]) * _dot(xb, wx_ref[...])
    if carry_mode:
        i = pl.program_id(1)

        @pl.when(i % (seq_len // tm) == 0)
        def _():
            carry_ref[...] = jnp.zeros_like(carry_ref)

        y = _conv3_carry(u, carry_ref[...], wconv_ref[...])
        carry_ref[...] = u[tm - SUBLANES:, :]
        tail_ref[...] = u[tm - SUBLANES:, :]
    else:
        y = _conv3_state(u, e_ref[...], wconv_ref[...], seq_len)
        tail_ref[...] = u
    z_ref[...] = (_dot(xb, wb_ref[...]) * y).astype(z_ref.dtype)
    q_ref[...] = _dot(xb, wq_ref[...]).astype(q_ref.dtype)
    k = _dot(xb, wk_ref[...])
    v = _dot(xb, wv_ref[...])
    if carry_mode:
        kt_ref[...] = k.T
        vt_ref[...] = v.T
        vb_ref[...] = v.astype(BF16)
    else:
        k_ref[...] = k
        v_ref[...] = v


def _in_proj(x, w_in_b, w_conv, layer, *, seq_len, q_dtype, state=None, kv_buffers=None):
    m, d = x.shape
    c = w_conv.shape[-1]
    carry_mode = state is None
    tm = min(IN_ROWS, m)
    tn = IN_COLS
    nb = c // tn
    batch = m // seq_len
    grid = (nb, m // tm)

    def wspec(s):
        return pl.BlockSpec((None, d, tn), lambda j, i, s=s: (layer, 0, s * nb + j))

    in_specs = [pl.BlockSpec((tm, d), lambda j, i: (i, 0))]
    in_specs += [wspec(s) for s in range(6)]
    in_specs += [pl.BlockSpec((None, w_conv.shape[1], tn), lambda j, i: (layer, 0, j))]
    args = [x] + [w_in_b] * 6 + [w_conv]
    tile = pl.BlockSpec((tm, tn), lambda j, i: (i, j))
    rows = lambda dt: jax.ShapeDtypeStruct((m, c), dt)
    scratch = []
    aliases = {}
    if carry_mode:
        assert seq_len % tm == 0
        per_seq = seq_len // tm
        tspec = pl.BlockSpec((None, None, tn, tm), lambda j, i: (layer, i // per_seq, j, i % per_seq))
        tshape = jax.ShapeDtypeStruct((w_in_b.shape[0], batch, c, seq_len), F32)
        if kv_buffers is not None:
            aliases = {len(args): 2, len(args) + 1: 3}
            in_specs += [pl.BlockSpec(memory_space=pl.ANY)] * 2
            args += list(kv_buffers)
        tail_shape = jax.ShapeDtypeStruct((m // tm, SUBLANES, c), F32)
        tail_spec = pl.BlockSpec((None, SUBLANES, tn), lambda j, i: (i, 0, j))
        scratch = [pltpu.VMEM((SUBLANES, tn), F32)]
        out_shape = (rows(BF16), rows(q_dtype), tshape, tshape, rows(BF16), tail_shape)
        out_specs = (tile, tile, tspec, tspec, tile, tail_spec)
    else:
        assert tm == m and seq_len == SUBLANES
        in_specs.append(tile)
        args.append(state)
        out_shape = (rows(BF16), rows(q_dtype), rows(F32), rows(F32), rows(F32))
        out_specs = (tile,) * 5
    return pl.pallas_call(
        functools.partial(_in_proj_kernel, seq_len=seq_len, carry_mode=carry_mode, n_aliased=len(aliases)),
        grid=grid, in_specs=in_specs, out_specs=out_specs, input_output_aliases=aliases,
        out_shape=out_shape, scratch_shapes=scratch, compiler_params=_params(2),
        name="in_proj")(*args)


def _attn_prompt_kernel(bias_ref, q_ref, kt_ref, v_ref, o_ref,
                        kb_ref, tri_ref, acc_ref, car_ref, s_ref, a_ref, *, layer, tk, head_dim, scale):
    c = pl.program_id(1)
    qi = pl.program_id(2)
    tq, width = q_ref.shape
    heads_per_tile = LANES // head_dim
    n_heads = width // head_dim
    nk = kt_ref.shape[1] // tk
    hs = range(n_heads)
    tile = lambda h: slice((h // heads_per_tile) * LANES, (h // heads_per_tile + 1) * LANES)

    @pl.when(qi == 0)
    def _():
        chan = lax.broadcasted_iota(jnp.int32, (LANES, tk), 0)
        for j in range(nk):
            for h in hs:
                lo = (h % heads_per_tile) * head_dim
                kt = kt_ref[tile(h), j * tk:(j + 1) * tk] * (scale * LOG2E)
                kb_ref[j, h] = jnp.where((chan >= lo) & (chan < lo + head_dim), kt, 0.0).astype(BF16)
        tri_ref[...] = _tri(tk)

    acc_ref[...] = jnp.zeros_like(acc_ref)
    qs = [q_ref[:, tile(h)] for h in hs]
    bias2 = [bias_ref[layer, c * n_heads + h] * LOG2E for h in hs]

    def logits(j):
        return [_dot(qs[h], kb_ref[j, h]) + bias2[h] for h in hs]

    tri = tri_ref[...]
    jn = jnp.maximum(qi - 1, 0)
    row = lax.broadcasted_iota(jnp.int32, (tq, tk), 0)
    col = lax.broadcasted_iota(jnp.int32, (tq, tk), 1)
    valid = col < row
    z_diag = logits(qi)
    z_next, locs = [], []
    for h in hs:
        lsn = jnp.where(valid, _log2_1m_beta(z_diag[h]), 0.0)
        z_next.append(_dot(qs[h], kb_ref[jn, h]) + bias2[h])
        locs.append(_suffix_sums(lsn, tri))
    for h in hs:
        a = jnp.where(valid, jnp.exp2(z_diag[h] + locs[h]), 0.0)
        car_ref[h] = jnp.broadcast_to(locs[h][:, 0:1], (tq, LANES))
        a_ref[h] = a.astype(BF16)
        s_ref[h] = z_next[h]

    def body(it, j_prev):
        j = qi - 1 - it
        jn = jnp.maximum(j - 1, 0)
        koff = pl.multiple_of(j_prev * tk, tk)
        tri = tri_ref[...]
        n = len(hs)
        locs = [None] * n

        def apply_prev(h):
            acc_ref[h] += _dot(a_ref[h], v_ref[pl.ds(koff, tk), tile(h)])

        def finish(h):
            carry = car_ref[h]
            a = jnp.exp2(s_ref[h] + locs[h] + jnp.tile(carry, (1, tk // LANES)))
            car_ref[h] = carry + jnp.broadcast_to(locs[h][:, 0:1], carry.shape)
            a_ref[h] = a.astype(BF16)

        apply_prev(0)
        for h in hs:
            lsn = _log2_1m_beta(s_ref[h])
            if h + 1 < n:
                apply_prev(h + 1)
            locs[h] = _suffix_sums(lsn, tri)
        for h in hs:
            z_next = _dot(qs[h], kb_ref[jn, h]) + bias2[h]
            finish(h)
            s_ref[h] = z_next
        return j

    j_last = lax.fori_loop(0, qi, body, qi)
    koff = pl.multiple_of(j_last * tk, tk)
    outs = [_dot(a_ref[h], v_ref[pl.ds(koff, tk), tile(h)]) for h in hs]

    lane = lax.broadcasted_iota(jnp.int32, (tq, LANES), 1)
    totals = [acc_ref[h] + outs[h] for h in hs]
    for t in range(width // LANES):
        out = totals[t * heads_per_tile]
        for h in range(1, heads_per_tile):
            out = jnp.where(lane >= h * head_dim, totals[t * heads_per_tile + h], out)
        o_ref[:, t * LANES:(t + 1) * LANES] = out.astype(o_ref.dtype)


def _attn_prompt(q, kt, v, sb_bias, layer, *, head_dim):
    m, w = q.shape
    _, batch, _, seq_len = kt.shape
    tq = tk = ATTN_BLOCK
    width = min(ATTN_WIDTH, w)
    n_heads = width // head_dim
    nq = seq_len // tq
    grid = (batch, w // width, nq)
    qspec = pl.BlockSpec((tq, width), lambda b, c, i: (b * nq + i, c))
    return pl.pallas_call(
        functools.partial(_attn_prompt_kernel, layer=layer, tk=tk, head_dim=head_dim,
                          scale=head_dim ** -0.5),
        grid=grid,
        in_specs=[pl.BlockSpec(memory_space=pltpu.SMEM), qspec,
                  pl.BlockSpec((None, None, width, seq_len), lambda b, c, i: (layer, b, c, 0)),
                  pl.BlockSpec((seq_len, width), lambda b, c, i: (b, c))],
        out_specs=qspec,
        out_shape=jax.ShapeDtypeStruct((m, w), BF16),
        scratch_shapes=[pltpu.VMEM((seq_len // tk, n_heads, LANES, tk), BF16),
                        pltpu.VMEM((tk, tk), BF16),
                        pltpu.VMEM((n_heads, tq, LANES), F32),
                        pltpu.VMEM((n_heads, tq, LANES), F32),
                        pltpu.VMEM((n_heads, tq, tk), F32),
                        pltpu.VMEM((n_heads, tq, tk), BF16)],
        compiler_params=_params(3), name="attn_prompt")(sb_bias, q, kt, v)


def _attn_sample_kernel(pt_ref, bias_ref, q_ref, kn_ref, vn_ref, *refs, heads, head_dim, scale, group):
    del pt_ref
    ck_refs, cv_refs = refs[:group], refs[group:2 * group]
    o_ref, wt_ref, tri_ref, acc_ref, car_ref = refs[2 * group:]
    j = pl.program_id(1)
    t, w = q_ref.shape
    page = ck_refs[0].shape[1]
    rows = heads * t

    def process(ks, vs, new_rows):
        wt, tri, bias2 = wt_ref[...], tri_ref[...], bias_ref[...]
        qk = _dot_nt if new_rows else _dot
        zs = [qk(wt, k.astype(BF16)) * (scale * LOG2E) + bias2 for k in ks]
        lsns = [_log2_1m_beta(z) for z in zs]
        if new_rows:
            row = lax.broadcasted_iota(jnp.int32, (rows, page), 0)
            col = lax.broadcasted_iota(jnp.int32, (rows, page), 1)
            valid = col < row % t
            lsns = [jnp.where(valid, lsn, 0.0) for lsn in lsns]
        locs = [_suffix_sums(lsn, tri) for lsn in lsns]
        carry = car_ref[...]
        avs = []
        for z, loc in zip(zs, locs):
            avs.append(jnp.exp2(z + loc + carry))
            carry = carry + jnp.broadcast_to(loc[:, 0:1], carry.shape)
        car_ref[...] = carry
        if new_rows:
            avs = [jnp.where(valid, a, 0.0) for a in avs]
        a_all = jnp.concatenate([a.astype(BF16) for a in avs], axis=1)
        if new_rows:
            acc_ref[...] += _dot(a_all, jnp.concatenate([v.astype(BF16) for v in vs], axis=0))
        else:
            acc_ref[...] += _dot_nt(a_all, jnp.concatenate([v.astype(BF16) for v in vs], axis=1))

    @pl.when(j == 0)
    def _():
        qt = jnp.tile(q_ref[...], (heads, 1))
        rowh = lax.broadcasted_iota(jnp.int32, (rows, w), 0) // t
        colh = lax.broadcasted_iota(jnp.int32, (rows, w), 1) // head_dim
        wt_ref[...] = jnp.where(rowh == colh, qt, 0.0).astype(BF16)
        tri_ref[...] = _tri(page)
        acc_ref[...] = jnp.zeros_like(acc_ref)
        car_ref[...] = jnp.zeros_like(car_ref)
        pad = jnp.zeros((page - t, w), F32)
        process([jnp.concatenate([kn_ref[...], pad], axis=0)],
                [jnp.concatenate([vn_ref[...], pad], axis=0)], True)

    @pl.when(j > 0)
    def _():
        process([r[...] for r in ck_refs], [r[...] for r in cv_refs], False)

    @pl.when(j == pl.num_programs(1) - 1)
    def _():
        heads_per_tile = LANES // head_dim
        lane = lax.broadcasted_iota(jnp.int32, (t, LANES), 1)
        for p in range(w // LANES):
            cols = slice(p * LANES, (p + 1) * LANES)
            r0 = p * heads_per_tile * t
            out = acc_ref[r0:r0 + t, cols]
            for h in range(1, heads_per_tile):
                out = jnp.where(lane >= h * head_dim, acc_ref[r0 + h * t:r0 + (h + 1) * t, cols], out)
            o_ref[:, cols] = out


def _attn_sample(q, k_new, v_new, cache_kt, cache_vt, page_table, bias_rows, layer, *, seq_len, head_dim):
    m, w = q.shape
    dbsz, n_pages = page_table.shape
    page = cache_kt.shape[3]
    heads = w // head_dim
    rows = heads * seq_len
    group = min(PAGE_GROUP, n_pages)
    assert page == LANES and seq_len == SUBLANES and n_pages % group == 0
    grid = (dbsz, n_pages // group + 1)
    rowspec = pl.BlockSpec((seq_len, w), lambda b, j, pt: (b, 0))

    def pagespec(g):
        return pl.BlockSpec(
            (None, None, w, page),
            lambda b, j, pt: (layer, pt[b, n_pages - 1 - ((jnp.maximum(j, 1) - 1) * group + g)], 0, 0))

    pagespecs = [pagespec(g) for g in range(group)]
    grid_spec = pltpu.PrefetchScalarGridSpec(
        num_scalar_prefetch=1, grid=grid,
        in_specs=[pl.BlockSpec((rows, LANES), lambda b, j, pt: (0, 0)),
                  rowspec, rowspec, rowspec] + pagespecs + pagespecs,
        out_specs=rowspec,
        scratch_shapes=[pltpu.VMEM((rows, w), BF16),
                        pltpu.VMEM((page, page), BF16),
                        pltpu.VMEM((rows, w), F32),
                        pltpu.VMEM((rows, LANES), F32)])
    return pl.pallas_call(
        functools.partial(_attn_sample_kernel, heads=heads, head_dim=head_dim, scale=head_dim ** -0.5,
                          group=group),
        grid_spec=grid_spec, out_shape=jax.ShapeDtypeStruct((m, w), F32),
        compiler_params=_params(2), name="attn_sample")(
            page_table, bias_rows, q, k_new, v_new, *([cache_kt] * group), *([cache_vt] * group))


def _out_proj_kernel(z_ref, o_ref, x_ref, p_ref, wout_ref, wpg_ref, wpe_ref, g_ref, b_ref,
                     x1b_ref, r_ref, *, alpha, row_chunk):
    chunks = [slice(r0, r0 + row_chunk) for r0 in range(0, x_ref.shape[0], row_chunk)]
    g, b = g_ref[...], b_ref[...]
    mixed = [jnp.concatenate([z_ref[rows, :].astype(BF16), o_ref[rows, :].astype(BF16)], axis=1)
             for rows in chunks]
    ys = [alpha * x_ref[rows, :] + _dot(mx, wout_ref[...]) for rows, mx in zip(chunks, mixed)]
    x1s = [_layer_norm(y, g, b) for y in ys]
    x1bs = [x1.astype(BF16) for x1 in x1s]
    logits = [_dot(x1b, wpg_ref[...]) for x1b in x1bs]
    pes = [_dot(p_ref[rows, :].astype(BF16), wpe_ref[...]) for rows in chunks]
    for rows, x1, x1b, lg, pe in zip(chunks, x1s, x1bs, logits, pes):
        x1b_ref[rows, :] = x1b
        r_ref[rows, :] = alpha * x1 + pe * (1.0 / (1.0 + jnp.exp(-lg)))


def _out_proj(z, o, x, p, w_out_b, w_pg_b, w_pe_b, ln_g, ln_b, layer, *, alpha):
    m, d = x.shape
    c = z.shape[1]
    pdim = p.shape[-1]
    tm = min(OUT_ROWS, m)
    rowspec = lambda width: pl.BlockSpec((tm, width), lambda i: (i, 0))
    const = lambda r, cc: pl.BlockSpec((None, r, cc), lambda i: (layer, 0, 0),
                                       pipeline_mode=pl.Buffered(1))
    return pl.pallas_call(
        functools.partial(_out_proj_kernel, alpha=alpha, row_chunk=min(OUT_ROW_CHUNK, tm)),
        grid=(m // tm,),
        in_specs=[rowspec(c), rowspec(c), rowspec(d),
                  pl.BlockSpec((None, tm, pdim), lambda i: (layer, i, 0)),
                  const(2 * c, d), const(d, d), const(pdim, d), const(1, d), const(1, d)],
        out_specs=(rowspec(d), rowspec(d)),
        out_shape=(jax.ShapeDtypeStruct((m, d), BF16), jax.ShapeDtypeStruct((m, d), F32)),
        compiler_params=_params(1), name="out_proj")(
            z, o, x, p, w_out_b, w_pg_b, w_pe_b, ln_g, ln_b)


def _ffn_kernel(*refs, seq_len, carry_mode, row_chunk):
    if carry_mode:
        (x1b_ref, r_ref, wa_ref, wg_ref, wconv_ref, bconv_ref, wd_ref, g_ref, b_ref,
         out_ref, tail_ref, carry_ref) = refs
    else:
        (x1b_ref, r_ref, wa_ref, wg_ref, wconv_ref, bconv_ref, wd_ref, g_ref, b_ref, e_ref,
         out_ref, tail_ref) = refs
    i = pl.program_id(0)
    f = pl.program_id(1)
    tm = x1b_ref.shape[0]

    @pl.when(f == 0)
    def _():
        out_ref[...] = r_ref[...]

    if carry_mode:
        @pl.when(i % (seq_len // tm) == 0)
        def _():
            carry_ref[f] = jnp.zeros(carry_ref.shape[1:], F32)

        prev8 = carry_ref[f]
    chunks = [slice(r0, r0 + row_chunk) for r0 in range(0, tm, row_chunk)]
    ups = [(_dot(x1b_ref[rows, :], wa_ref[...]), _dot(x1b_ref[rows, :], wg_ref[...])) for rows in chunks]
    hs = []
    for rows, (a, gate) in zip(chunks, ups):
        if carry_mode:
            a_c = _conv3_carry(a, prev8, wconv_ref[...])
            prev8 = a[row_chunk - SUBLANES:, :]
        else:
            a_c = _conv3_state(a, e_ref[rows, :], wconv_ref[...], seq_len)
            tail_ref[rows, :] = a
        hs.append((_gelu_tanh(a_c + bconv_ref[...]) * gate).astype(BF16))
    for rows, h in zip(chunks, hs):
        out_ref[rows, :] += _dot(h, wd_ref[...])
    if carry_mode:
        carry_ref[f] = prev8
        tail_ref[...] = prev8

    @pl.when(f == pl.num_programs(1) - 1)
    def _():
        out_ref[...] = _layer_norm(out_ref[...], g_ref[...], b_ref[...])


def _ffn(x1b, r, w_up_t, w_conv, b_conv, w_down_b, ln_g, ln_b, layer, *, seq_len, state=None):
    m, d = r.shape
    dff = w_conv.shape[-1]
    carry_mode = state is None
    tf = w_up_t.shape[-1]
    nf = dff // tf
    tm = min(FFN_ROWS, m)
    assert w_up_t.shape[1] == 2 * nf
    rowspec = pl.BlockSpec((tm, d), lambda i, f: (i, 0))
    vec = lambda rows, width: pl.BlockSpec((None, rows, width), lambda i, f: (layer, 0, 0))
    in_specs = [rowspec, rowspec,
                pl.BlockSpec((None, None, d, tf), lambda i, f: (layer, f, 0, 0)),
                pl.BlockSpec((None, None, d, tf), lambda i, f: (layer, nf + f, 0, 0)),
                pl.BlockSpec((None, w_conv.shape[1], tf), lambda i, f: (layer, 0, f)),
                pl.BlockSpec((None, 1, tf), lambda i, f: (layer, 0, f)),
                pl.BlockSpec((None, tf, d), lambda i, f: (layer, f, 0)),
                vec(1, d), vec(1, d)]
    args = [x1b, r, w_up_t, w_up_t, w_conv, b_conv, w_down_b, ln_g, ln_b]
    scratch = []
    if carry_mode:
        assert seq_len % tm == 0
        tail_shape = jax.ShapeDtypeStruct((m // tm, SUBLANES, dff), F32)
        tail_spec = pl.BlockSpec((None, SUBLANES, tf), lambda i, f: (i, 0, f))
        scratch = [pltpu.VMEM((nf, SUBLANES, tf), F32)]
    else:
        assert tm == m and seq_len == SUBLANES
        tile = pl.BlockSpec((tm, tf), lambda i, f: (i, f))
        in_specs.append(tile)
        args.append(state)
        tail_shape = jax.ShapeDtypeStruct((m, dff), F32)
        tail_spec = tile
    return pl.pallas_call(
        functools.partial(_ffn_kernel, seq_len=seq_len, carry_mode=carry_mode,
                          row_chunk=min(FFN_ROW_CHUNK, tm)),
        grid=(m // tm, nf), in_specs=in_specs, out_specs=(rowspec, tail_spec),
        out_shape=(jax.ShapeDtypeStruct((m, d), F32), tail_shape),
        scratch_shapes=scratch, compiler_params=_params(2), name="ffn")(*args)


def _state_rows(state, seq_len):
    b, k1, c = state.shape
    return jnp.pad(state, ((0, 0), (0, seq_len - k1), (0, 0))).reshape(b * seq_len, c)


def _last_rows(tails, batch, n):
    per_seq = tails.shape[0] // batch
    return tails[per_seq - 1::per_seq, SUBLANES - n:]


def kernel(x_prompt, x_sample, cache_k, cache_v, state_conv_mix, state_conv_ffn, page_table, p_prompt,
           p_sample, w_in, w_conv_mix, sb_bias, w_out, ln1_g, ln1_b, w_up, w_ffn_conv, b_ffn_conv,
           w_down, w_pg, w_pe, ln2_g, ln2_b):
    bsz, seq, d = x_prompt.shape
    dbsz, dseq, _ = x_sample.shape
    depth, n_pool, page, heads, head_dim = cache_k.shape
    width = heads * head_dim
    conv_k = w_conv_mix.shape[1]
    assert conv_k == 3 and w_ffn_conv.shape[1] == 3
    alpha = (2.0 * depth) ** 0.25

    w_in_b, w_out_b = w_in.astype(BF16), w_out.astype(BF16)
    w_up_t = w_up.reshape(depth, d, -1, FFN_COLS).transpose(0, 2, 1, 3).astype(BF16)
    w_down_b, w_pg_b, w_pe_b = w_down.astype(BF16), w_pg.astype(BF16), w_pe.astype(BF16)
    vec3 = lambda a: a.reshape(depth, 1, a.shape[-1])
    ln1_g3, ln1_b3, ln2_g3, ln2_b3, b_conv3 = map(vec3, (ln1_g, ln1_b, ln2_g, ln2_b, b_ffn_conv))
    cache_kt = cache_k.transpose(0, 1, 3, 4, 2).reshape(depth, n_pool, width, page)
    cache_vt = cache_v.transpose(0, 1, 3, 4, 2).reshape(depth, n_pool, width, page)
    pp = p_prompt.reshape(depth, bsz * seq, -1)
    ps = p_sample.reshape(depth, dbsz * dseq, -1)
    bias_rows = jnp.broadcast_to((sb_bias * LOG2E)[:, :, None, None], (depth, heads, dseq, LANES)
                                 ).reshape(depth, heads * dseq, LANES)

    xp = x_prompt.reshape(bsz * seq, d)
    xs = x_sample.reshape(dbsz * dseq, d)
    outs = [[] for _ in range(6)]
    kv = None
    for l in range(depth):
        shared_o = (w_out_b, w_pg_b, w_pe_b, ln1_g3, ln1_b3, l)
        shared_f = (w_up_t, w_ffn_conv, b_conv3, w_down_b, ln2_g3, ln2_b3, l)
        z, q, *kv, vb, mix_tail = _in_proj(xp, w_in_b, w_conv_mix, l, seq_len=seq, q_dtype=BF16,
                                           kv_buffers=kv)
        o = _attn_prompt(q, kv[0], vb, sb_bias, l, head_dim=head_dim)
        x1b, r = _out_proj(z, o, xp, pp, *shared_o, alpha=alpha)
        xp, ffn_tail = _ffn(x1b, r, *shared_f, seq_len=seq)
        outs[0].append(_last_rows(mix_tail, bsz, conv_k - 1))
        outs[1].append(_last_rows(ffn_tail, bsz, conv_k - 1))
        z, q, k, v, mix_tail = _in_proj(xs, w_in_b, w_conv_mix, l, seq_len=dseq, q_dtype=F32,
                                        state=_state_rows(state_conv_mix[l], dseq))
        o = _attn_sample(q, k, v, cache_kt, cache_vt, page_table, bias_rows[l], l,
                         seq_len=dseq, head_dim=head_dim)
        x1b, r = _out_proj(z, o, xs, ps, *shared_o, alpha=alpha)
        xs, ffn_tail = _ffn(x1b, r, *shared_f, seq_len=dseq, state=_state_rows(state_conv_ffn[l], dseq))
        outs[2].append(k.reshape(dbsz, dseq, heads, head_dim))
        outs[3].append(v.reshape(dbsz, dseq, heads, head_dim))
        outs[4].append(mix_tail.reshape(dbsz, dseq, -1)[:, dseq - (conv_k - 1):])
        outs[5].append(ffn_tail.reshape(dbsz, dseq, -1)[:, dseq - (conv_k - 1):])
    kv_prompt = [a.reshape(depth, bsz, heads, head_dim, seq).transpose(0, 1, 4, 2, 3) for a in kv]
    stacked = [jnp.stack(o) for o in outs]
    return (xp.reshape(bsz, seq, d), xs.reshape(dbsz, dseq, d), *kv_prompt, *stacked)
```

```python
import functools

import jax
import jax.numpy as jnp
from jax import lax
from jax.experimental import pallas as pl
from jax.experimental.pallas import tpu as pltpu

BF16 = jnp.bfloat16
F32 = jnp.float32

LN_EPS = 1e-5
LANES = 128
SUBLANES = 8
VMEM_LIMIT = 56 * 1024 * 1024

IN_ROWS = 512
IN_COLS = 512
ATTN_BLOCK = 256
ATTN_WIDTH = 8 * LANES
PAGE_GROUP = 16
OUT_ROWS = 512
OUT_ROW_CHUNK = 256
FFN_COLS = 256
FFN_ROWS = 1024
FFN_ROW_CHUNK = 512

_dot = functools.partial(jnp.dot, preferred_element_type=F32)


def _dot_nt(a, b):
    return lax.dot_general(a, b, (((1,), (1,)), ((), ())), preferred_element_type=F32)


def _params(n_grid):
    return pltpu.CompilerParams(dimension_semantics=("arbitrary",) * n_grid,
                                vmem_limit_bytes=VMEM_LIMIT)


def _shift_rows_carry(u, prev8, shift):
    p = pltpu.roll(u, shift, axis=0)
    row8 = lax.broadcasted_iota(jnp.int32, prev8.shape, 0)
    top = jnp.where(row8 < shift, pltpu.roll(prev8, shift, axis=0), p[0:SUBLANES])
    return jnp.concatenate([top, p[SUBLANES:]], axis=0)


def _conv3_carry(u, prev8, w):
    return (w[0:1] * _shift_rows_carry(u, prev8, 2) + w[1:2] * _shift_rows_carry(u, prev8, 1)
            + w[2:3] * u)


def _conv3_state(u, e, w, seq_len):
    tm = u.shape[0]
    pos = lax.broadcasted_iota(jnp.int32, u.shape, 0) % seq_len
    p1 = jnp.where(pos >= 1, pltpu.roll(u, 1, axis=0), pltpu.roll(e, tm - 1, axis=0))
    p2 = jnp.where(pos >= 2, pltpu.roll(u, 2, axis=0), e)
    return w[0:1] * p2 + w[1:2] * p1 + w[2:3] * u


def _layer_norm(y, g, b):
    mu = jnp.mean(y, axis=-1, keepdims=True)
    d = y - mu
    var = jnp.mean(d * d, axis=-1, keepdims=True)
    return d * lax.rsqrt(var + LN_EPS) * g + b


def _gelu_tanh(x):
    return x * (0.5 * (1.0 + jnp.tanh(0.7978845608028654 * (x + 0.044715 * (x * x * x)))))


LOG2E = 1.4426950408889634


def _log2_1m_beta(z2):
    nz = -z2
    return jnp.minimum(nz, 0.0) - jnp.log(1.0 + jnp.exp2(jnp.minimum(z2, nz))) * LOG2E


def _suffix_sums(lsn, tri):
    return _dot(lsn.astype(BF16), tri)


def _tri(n):
    r = lax.broadcasted_iota(jnp.int32, (n, n), 0)
    c = lax.broadcasted_iota(jnp.int32, (n, n), 1)
    return jnp.where(r >= c, 1.0, 0.0).astype(BF16)


def _in_proj_kernel(*refs, seq_len, carry_mode, n_aliased):
    if carry_mode:
        refs = refs[:8] + refs[8 + n_aliased:]
        (x_ref, wb_ref, wc_ref, wx_ref, wq_ref, wk_ref, wv_ref, wconv_ref,
         z_ref, q_ref, kt_ref, vt_ref, vb_ref, tail_ref, carry_ref) = refs
    else:
        (x_ref, wb_ref, wc_ref, wx_ref, wq_ref, wk_ref, wv_ref, wconv_ref, e_ref,
         z_ref, q_ref, k_ref, v_ref, tail_ref) = refs
    tm = x_ref.shape[0]
    xb = x_ref[...].astype(BF16)
    u = _dot(xb, wc_ref[...]) * _dot(xb, wx_ref[...])
    if carry_mode:
        i = pl.program_id(1)

        @pl.when(i % (seq_len // tm) == 0)
        def _():
            carry_ref[...] = jnp.zeros_like(carry_ref)

        y = _conv3_carry(u, carry_ref[...], wconv_ref[...])
        carry_ref[...] = u[tm - SUBLANES:, :]
        tail_ref[...] = u[tm - SUBLANES:, :]
    else:
        y = _conv3_state(u, e_ref[...], wconv_ref[...], seq_len)
        tail_ref[...] = u
    z_ref[...] = (_dot(xb, wb_ref[...]) * y).astype(z_ref.dtype)
    q_ref[...] = _dot(xb, wq_ref[...]).astype(q_ref.dtype)
    k = _dot(xb, wk_ref[...])
    v = _dot(xb, wv_ref[...])
    if carry_mode:
        kt_ref[...] = k.T
        vt_ref[...] = v.T
        vb_ref[...] = v.astype(BF16)
    else:
        k_ref[...] = k
        v_ref[...] = v


def _in_proj(x, w_in_b, w_conv, layer, *, seq_len, q_dtype, state=None, kv_buffers=None):
    m, d = x.shape
    c = w_conv.shape[-1]
    carry_mode = state is None
    tm = min(IN_ROWS, m)
    tn = IN_COLS
    nb = c // tn
    batch = m // seq_len
    grid = (nb, m // tm)

    def wspec(s):
        return pl.BlockSpec((None, d, tn), lambda j, i, s=s: (layer, 0, s * nb + j))

    in_specs = [pl.BlockSpec((tm, d), lambda j, i: (i, 0))]
    in_specs += [wspec(s) for s in range(6)]
    in_specs += [pl.BlockSpec((None, w_conv.shape[1], tn), lambda j, i: (layer, 0, j))]
    args = [x] + [w_in_b] * 6 + [w_conv]
    tile = pl.BlockSpec((tm, tn), lambda j, i: (i, j))
    rows = lambda dt: jax.ShapeDtypeStruct((m, c), dt)
    scratch = []
    aliases = {}
    if carry_mode:
        assert seq_len % tm == 0
        per_seq = seq_len // tm
        tspec = pl.BlockSpec((None, None, tn, tm), lambda j, i: (layer, i // per_seq, j, i % per_seq))
        tshape = jax.ShapeDtypeStruct((w_in_b.shape[0], batch, c, seq_len), F32)
        if kv_buffers is not None:
            aliases = {len(args): 2, len(args) + 1: 3}
            in_specs += [pl.BlockSpec(memory_space=pl.ANY)] * 2
            args += list(kv_buffers)
        tail_shape = jax.ShapeDtypeStruct((m // tm, SUBLANES, c), F32)
        tail_spec = pl.BlockSpec((None, SUBLANES, tn), lambda j, i: (i, 0, j))
        scratch = [pltpu.VMEM((SUBLANES, tn), F32)]
        out_shape = (rows(BF16), rows(q_dtype), tshape, tshape, rows(BF16), tail_shape)
        out_specs = (tile, tile, tspec, tspec, tile, tail_spec)
    else:
        assert tm == m and seq_len == SUBLANES
        in_specs.append(tile)
        args.append(state)
        out_shape = (rows(BF16), rows(q_dtype), rows(F32), rows(F32), rows(F32))
        out_specs = (tile,) * 5
    return pl.pallas_call(
        functools.partial(_in_proj_kernel, seq_len=seq_len, carry_mode=carry_mode, n_aliased=len(aliases)),
        grid=grid, in_specs=in_specs, out_specs=out_specs, input_output_aliases=aliases,
        out_shape=out_shape, scratch_shapes=scratch, compiler_params=_params(2),
        name="in_proj")(*args)


def _attn_prompt_kernel(bias_ref, q_ref, kt_ref, v_ref, o_ref,
                        kb_ref, tri_ref, acc_ref, car_ref, s_ref, a_ref, *, layer, tk, head_dim, scale):
    c = pl.program_id(1)
    qi = pl.program_id(2)
    tq, width = q_ref.shape
    heads_per_tile = LANES // head_dim
    n_heads = width // head_dim
    nk = kt_ref.shape[1] // tk
    hs = range(n_heads)
    tile = lambda h: slice((h // heads_per_tile) * LANES, (h // heads_per_tile + 1) * LANES)

    @pl.when(qi == 0)
    def _():
        chan = lax.broadcasted_iota(jnp.int32, (LANES, tk), 0)
        for j in range(nk):
            for h in hs:
                lo = (h % heads_per_tile) * head_dim
                kt = kt_ref[tile(h), j * tk:(j + 1) * tk] * (scale * LOG2E)
                kb_ref[j, h] = jnp.where((chan >= lo) & (chan < lo + head_dim), kt, 0.0).astype(BF16)
        tri_ref[...] = _tri(tk)

    acc_ref[...] = jnp.zeros_like(acc_ref)
    qs = [q_ref[:, tile(h)] for h in hs]
    bias2 = [bias_ref[layer, c * n_heads + h] * LOG2E for h in hs]

    def logits(j):
        return [_dot(qs[h], kb_ref[j, h]) + bias2[h] for h in hs]

    tri = tri_ref[...]
    jn = jnp.maximum(qi - 1, 0)
    row = lax.broadcasted_iota(jnp.int32, (tq, tk), 0)
    col = lax.broadcasted_iota(jnp.int32, (tq, tk), 1)
    valid = col < row
    z_diag = logits(qi)
    z_next, locs = [], []
    for h in hs:
        lsn = jnp.where(valid, _log2_1m_beta(z_diag[h]), 0.0)
        z_next.append(_dot(qs[h], kb_ref[jn, h]) + bias2[h])
        locs.append(_suffix_sums(lsn, tri))
    for h in hs:
        a = jnp.where(valid, jnp.exp2(z_diag[h] + locs[h]), 0.0)
        car_ref[h] = jnp.broadcast_to(locs[h][:, 0:1], (tq, LANES))
        a_ref[h] = a.astype(BF16)
        s_ref[h] = z_next[h]

    def body(it, j_prev):
        j = qi - 1 - it
        jn = jnp.maximum(j - 1, 0)
        koff = pl.multiple_of(j_prev * tk, tk)
        tri = tri_ref[...]
        n = len(hs)
        locs = [None] * n

        def apply_prev(h):
            acc_ref[h] += _dot(a_ref[h], v_ref[pl.ds(koff, tk), tile(h)])

        def finish(h):
            carry = car_ref[h]
            a = jnp.exp2(s_ref[h] + locs[h] + jnp.tile(carry, (1, tk // LANES)))
            car_ref[h] = carry + jnp.broadcast_to(locs[h][:, 0:1], carry.shape)
            a_ref[h] = a.astype(BF16)

        apply_prev(0)
        for h in hs:
            lsn = _log2_1m_beta(s_ref[h])
            if h + 1 < n:
                apply_prev(h + 1)
            locs[h] = _suffix_sums(lsn, tri)
        for h in hs:
            z_next = _dot(qs[h], kb_ref[jn, h]) + bias2[h]
            finish(h)
            s_ref[h] = z_next
        return j

    j_last = lax.fori_loop(0, qi, body, qi)
    koff = pl.multiple_of(j_last * tk, tk)
    outs = [_dot(a_ref[h], v_ref[pl.ds(koff, tk), tile(h)]) for h in hs]

    lane = lax.broadcasted_iota(jnp.int32, (tq, LANES), 1)
    totals = [acc_ref[h] + outs[h] for h in hs]
    for t in range(width // LANES):
        out = totals[t * heads_per_tile]
        for h in range(1, heads_per_tile):
            out = jnp.where(lane >= h * head_dim, totals[t * heads_per_tile + h], out)
        o_ref[:, t * LANES:(t + 1) * LANES] = out.astype(o_ref.dtype)


def _attn_prompt(q, kt, v, sb_bias, layer, *, head_dim):
    m, w = q.shape
    _, batch, _, seq_len = kt.shape
    tq = tk = ATTN_BLOCK
    width = min(ATTN_WIDTH, w)
    n_heads = width // head_dim
    nq = seq_len // tq
    grid = (batch, w // width, nq)
    qspec = pl.BlockSpec((tq, width), lambda b, c, i: (b * nq + i, c))
    return pl.pallas_call(
        functools.partial(_attn_prompt_kernel, layer=layer, tk=tk, head_dim=head_dim,
                          scale=head_dim ** -0.5),
        grid=grid,
        in_specs=[pl.BlockSpec(memory_space=pltpu.SMEM), qspec,
                  pl.BlockSpec((None, None, width, seq_len), lambda b, c, i: (layer, b, c, 0)),
                  pl.BlockSpec((seq_len, width), lambda b, c, i: (b, c))],
        out_specs=qspec,
        out_shape=jax.ShapeDtypeStruct((m, w), BF16),
        scratch_shapes=[pltpu.VMEM((seq_len // tk, n_heads, LANES, tk), BF16),
                        pltpu.VMEM((tk, tk), BF16),
                        pltpu.VMEM((n_heads, tq, LANES), F32),
                        pltpu.VMEM((n_heads, tq, LANES), F32),
                        pltpu.VMEM((n_heads, tq, tk), F32),
                        pltpu.VMEM((n_heads, tq, tk), BF16)],
        compiler_params=_params(3), name="attn_prompt")(sb_bias, q, kt, v)


def _attn_sample_kernel(pt_ref, bias_ref, q_ref, kn_ref, vn_ref, *refs, heads, head_dim, scale, group):
    del pt_ref
    ck_refs, cv_refs = refs[:group], refs[group:2 * group]
    o_ref, wt_ref, tri_ref, acc_ref, car_ref = refs[2 * group:]
    j = pl.program_id(1)
    t, w = q_ref.shape
    page = ck_refs[0].shape[1]
    rows = heads * t

    def process(ks, vs, new_rows):
        wt, tri, bias2 = wt_ref[...], tri_ref[...], bias_ref[...]
        qk = _dot_nt if new_rows else _dot
        zs = [qk(wt, k.astype(BF16)) * (scale * LOG2E) + bias2 for k in ks]
        lsns = [_log2_1m_beta(z) for z in zs]
        if new_rows:
            row = lax.broadcasted_iota(jnp.int32, (rows, page), 0)
            col = lax.broadcasted_iota(jnp.int32, (rows, page), 1)
            valid = col < row % t
            lsns = [jnp.where(valid, lsn, 0.0) for lsn in lsns]
        locs = [_suffix_sums(lsn, tri) for lsn in lsns]
        carry = car_ref[...]
        avs = []
        for z, loc in zip(zs, locs):
            avs.append(jnp.exp2(z + loc + carry))
            carry = carry + jnp.broadcast_to(loc[:, 0:1], carry.shape)
        car_ref[...] = carry
        if new_rows:
            avs = [jnp.where(valid, a, 0.0) for a in avs]
        a_all = jnp.concatenate([a.astype(BF16) for a in avs], axis=1)
        if new_rows:
            acc_ref[...] += _dot(a_all, jnp.concatenate([v.astype(BF16) for v in vs], axis=0))
        else:
            acc_ref[...] += _dot_nt(a_all, jnp.concatenate([v.astype(BF16) for v in vs], axis=1))

    @pl.when(j == 0)
    def _():
        qt = jnp.tile(q_ref[...], (heads, 1))
        rowh = lax.broadcasted_iota(jnp.int32, (rows, w), 0) // t
        colh = lax.broadcasted_iota(jnp.int32, (rows, w), 1) // head_dim
        wt_ref[...] = jnp.where(rowh == colh, qt, 0.0).astype(BF16)
        tri_ref[...] = _tri(page)
        acc_ref[...] = jnp.zeros_like(acc_ref)
        car_ref[...] = jnp.zeros_like(car_ref)
        pad = jnp.zeros((page - t, w), F32)
        process([jnp.concatenate([kn_ref[...], pad], axis=0)],
                [jnp.concatenate([vn_ref[...], pad], axis=0)], True)

    @pl.when(j > 0)
    def _():
        process([r[...] for r in ck_refs], [r[...] for r in cv_refs], False)

    @pl.when(j == pl.num_programs(1) - 1)
    def _():
        heads_per_tile = LANES // head_dim
        lane = lax.broadcasted_iota(jnp.int32, (t, LANES), 1)
        for p in range(w // LANES):
            cols = slice(p * LANES, (p + 1) * LANES)
            r0 = p * heads_per_tile * t
            out = acc_ref[r0:r0 + t, cols]
            for h in range(1, heads_per_tile):
                out = jnp.where(lane >= h * head_dim, acc_ref[r0 + h * t:r0 + (h + 1) * t, cols], out)
            o_ref[:, cols] = out


def _attn_sample(q, k_new, v_new, cache_kt, cache_vt, page_table, bias_rows, layer, *, seq_len, head_dim):
    m, w = q.shape
    dbsz, n_pages = page_table.shape
    page = cache_kt.shape[3]
    heads = w // head_dim
    rows = heads * seq_len
    group = min(PAGE_GROUP, n_pages)
    assert page == LANES and seq_len == SUBLANES and n_pages % group == 0
    grid = (dbsz, n_pages // group + 1)
    rowspec = pl.BlockSpec((seq_len, w), lambda b, j, pt: (b, 0))

    def pagespec(g):
        return pl.BlockSpec(
            (None, None, w, page),
            lambda b, j, pt: (layer, pt[b, n_pages - 1 - ((jnp.maximum(j, 1) - 1) * group + g)], 0, 0))

    pagespecs = [pagespec(g) for g in range(group)]
    grid_spec = pltpu.PrefetchScalarGridSpec(
        num_scalar_prefetch=1, grid=grid,
        in_specs=[pl.BlockSpec((rows, LANES), lambda b, j, pt: (0, 0)),
                  rowspec, rowspec, rowspec] + pagespecs + pagespecs,
        out_specs=rowspec,
        scratch_shapes=[pltpu.VMEM((rows, w), BF16),
                        pltpu.VMEM((page, page), BF16),
                        pltpu.VMEM((rows, w), F32),
                        pltpu.VMEM((rows, LANES), F32)])
    return pl.pallas_call(
        functools.partial(_attn_sample_kernel, heads=heads, head_dim=head_dim, scale=head_dim ** -0.5,
                          group=group),
        grid_spec=grid_spec, out_shape=jax.ShapeDtypeStruct((m, w), F32),
        compiler_params=_params(2), name="attn_sample")(
            page_table, bias_rows, q, k_new, v_new, *([cache_kt] * group), *([cache_vt] * group))


def _out_proj_kernel(z_ref, o_ref, x_ref, p_ref, wout_ref, wpg_ref, wpe_ref, g_ref, b_ref,
                     x1b_ref, r_ref, *, alpha, row_chunk):
    chunks = [slice(r0, r0 + row_chunk) for r0 in range(0, x_ref.shape[0], row_chunk)]
    g, b = g_ref[...], b_ref[...]
    mixed = [jnp.concatenate([z_ref[rows, :].astype(BF16), o_ref[rows, :].astype(BF16)], axis=1)
             for rows in chunks]
    ys = [alpha * x_ref[rows, :] + _dot(mx, wout_ref[...]) for rows, mx in zip(chunks, mixed)]
    x1s = [_layer_norm(y, g, b) for y in ys]
    x1bs = [x1.astype(BF16) for x1 in x1s]
    logits = [_dot(x1b, wpg_ref[...]) for x1b in x1bs]
    pes = [_dot(p_ref[rows, :].astype(BF16), wpe_ref[...]) for rows in chunks]
    for rows, x1, x1b, lg, pe in zip(chunks, x1s, x1bs, logits, pes):
        x1b_ref[rows, :] = x1b
        r_ref[rows, :] = alpha * x1 + pe * (1.0 / (1.0 + jnp.exp(-lg)))


def _out_proj(z, o, x, p, w_out_b, w_pg_b, w_pe_b, ln_g, ln_b, layer, *, alpha):
    m, d = x.shape
    c = z.shape[1]
    pdim = p.shape[-1]
    tm = min(OUT_ROWS, m)
    rowspec = lambda width: pl.BlockSpec((tm, width), lambda i: (i, 0))
    const = lambda r, cc: pl.BlockSpec((None, r, cc), lambda i: (layer, 0, 0),
                                       pipeline_mode=pl.Buffered(1))
    return pl.pallas_call(
        functools.partial(_out_proj_kernel, alpha=alpha, row_chunk=min(OUT_ROW_CHUNK, tm)),
        grid=(m // tm,),
        in_specs=[rowspec(c), rowspec(c), rowspec(d),
                  pl.BlockSpec((None, tm, pdim), lambda i: (layer, i, 0)),
                  const(2 * c, d), const(d, d), const(pdim, d), const(1, d), const(1, d)],
        out_specs=(rowspec(d), rowspec(d)),
        out_shape=(jax.ShapeDtypeStruct((m, d), BF16), jax.ShapeDtypeStruct((m, d), F32)),
        compiler_params=_params(1), name="out_proj")(
            z, o, x, p, w_out_b, w_pg_b, w_pe_b, ln_g, ln_b)


def _ffn_kernel(*refs, seq_len, carry_mode, row_chunk):
    if carry_mode:
        (x1b_ref, r_ref, wa_ref, wg_ref, wconv_ref, bconv_ref, wd_ref, g_ref, b_ref,
         out_ref, tail_ref, carry_ref) = refs
    else:
        (x1b_ref, r_ref, wa_ref, wg_ref, wconv_ref, bconv_ref, wd_ref, g_ref, b_ref, e_ref,
         out_ref, tail_ref) = refs
    i = pl.program_id(0)
    f = pl.program_id(1)
    tm = x1b_ref.shape[0]

    @pl.when(f == 0)
    def _():
        out_ref[...] = r_ref[...]

    if carry_mode:
        @pl.when(i % (seq_len // tm) == 0)
        def _():
            carry_ref[f] = jnp.zeros(carry_ref.shape[1:], F32)

        prev8 = carry_ref[f]
    chunks = [slice(r0, r0 + row_chunk) for r0 in range(0, tm, row_chunk)]
    tf = wa_ref.shape[1]
    cols = pl.ds(pl.multiple_of(f * tf, tf), tf)
    wconv, bconv = wconv_ref[:, cols], bconv_ref[:, cols]
    ups = [(_dot(x1b_ref[rows, :], wa_ref[...]), _dot(x1b_ref[rows, :], wg_ref[...])) for rows in chunks]
    hs = []
    for rows, (a, gate) in zip(chunks, ups):
        if carry_mode:
            a_c = _conv3_carry(a, prev8, wconv)
            prev8 = a[row_chunk - SUBLANES:, :]
        else:
            a_c = _conv3_state(a, e_ref[rows, :], wconv, seq_len)
            tail_ref[rows, :] = a
        hs.append((_gelu_tanh(a_c + bconv) * gate).astype(BF16))
    for rows, h in zip(chunks, hs):
        out_ref[rows, :] += _dot(h, wd_ref[...])
    if carry_mode:
        carry_ref[f] = prev8
        tail_ref[:, cols] = prev8

    @pl.when(f == pl.num_programs(1) - 1)
    def _():
        out_ref[...] = _layer_norm(out_ref[...], g_ref[...], b_ref[...])


def _ffn(x1b, r, w_up_t, w_conv, b_conv, w_down_b, ln_g, ln_b, layer, *, seq_len, state=None):
    m, d = r.shape
    dff = w_conv.shape[-1]
    carry_mode = state is None
    tf = w_up_t.shape[-1]
    nf = dff // tf
    tm = min(FFN_ROWS, m)
    assert w_up_t.shape[1] == 2 * nf
    rowspec = pl.BlockSpec((tm, d), lambda i, f: (i, 0))
    vec = lambda rows, width: pl.BlockSpec((None, rows, width), lambda i, f: (layer, 0, 0))
    in_specs = [rowspec, rowspec,
                pl.BlockSpec((None, None, d, tf), lambda i, f: (layer, f, 0, 0)),
                pl.BlockSpec((None, None, d, tf), lambda i, f: (layer, nf + f, 0, 0)),
                vec(w_conv.shape[1], dff), vec(1, dff),
                pl.BlockSpec((None, tf, d), lambda i, f: (layer, f, 0)),
                vec(1, d), vec(1, d)]
    args = [x1b, r, w_up_t, w_up_t, w_conv, b_conv, w_down_b, ln_g, ln_b]
    scratch = []
    if carry_mode:
        assert seq_len % tm == 0
        tail_shape = jax.ShapeDtypeStruct((m // tm, SUBLANES, dff), F32)
        tail_spec = pl.BlockSpec((None, SUBLANES, dff), lambda i, f: (i, 0, 0))
        scratch = [pltpu.VMEM((nf, SUBLANES, tf), F32)]
    else:
        assert tm == m and seq_len == SUBLANES
        tile = pl.BlockSpec((tm, tf), lambda i, f: (i, f))
        in_specs.append(tile)
        args.append(state)
        tail_shape = jax.ShapeDtypeStruct((m, dff), F32)
        tail_spec = tile
    return pl.pallas_call(
        functools.partial(_ffn_kernel, seq_len=seq_len, carry_mode=carry_mode,
                          row_chunk=min(FFN_ROW_CHUNK, tm)),
        grid=(m // tm, nf), in_specs=in_specs, out_specs=(rowspec, tail_spec),
        out_shape=(jax.ShapeDtypeStruct((m, d), F32), tail_shape),
        scratch_shapes=scratch, compiler_params=_params(2), name="ffn")(*args)


def _state_rows(state, seq_len):
    b, k1, c = state.shape
    return jnp.pad(state, ((0, 0), (0, seq_len - k1), (0, 0))).reshape(b * seq_len, c)


def _last_rows(tails, batch, n):
    per_seq = tails.shape[0] // batch
    return tails[per_seq - 1::per_seq, SUBLANES - n:]


def kernel(x_prompt, x_sample, cache_k, cache_v, state_conv_mix, state_conv_ffn, page_table, p_prompt,
           p_sample, w_in, w_conv_mix, sb_bias, w_out, ln1_g, ln1_b, w_up, w_ffn_conv, b_ffn_conv,
           w_down, w_pg, w_pe, ln2_g, ln2_b):
    bsz, seq, d = x_prompt.shape
    dbsz, dseq, _ = x_sample.shape
    depth, n_pool, page, heads, head_dim = cache_k.shape
    width = heads * head_dim
    conv_k = w_conv_mix.shape[1]
    assert conv_k == 3 and w_ffn_conv.shape[1] == 3
    alpha = (2.0 * depth) ** 0.25

    w_in_b, w_out_b = w_in.astype(BF16), w_out.astype(BF16)
    w_up_t = w_up.reshape(depth, d, -1, FFN_COLS).transpose(0, 2, 1, 3).astype(BF16)
    w_down_b, w_pg_b, w_pe_b = w_down.astype(BF16), w_pg.astype(BF16), w_pe.astype(BF16)
    vec3 = lambda a: a.reshape(depth, 1, a.shape[-1])
    ln1_g3, ln1_b3, ln2_g3, ln2_b3, b_conv3 = map(vec3, (ln1_g, ln1_b, ln2_g, ln2_b, b_ffn_conv))
    cache_kt = cache_k.transpose(0, 1, 3, 4, 2).reshape(depth, n_pool, width, page)
    cache_vt = cache_v.transpose(0, 1, 3, 4, 2).reshape(depth, n_pool, width, page)
    pp = p_prompt.reshape(depth, bsz * seq, -1)
    ps = p_sample.reshape(depth, dbsz * dseq, -1)
    bias_rows = jnp.broadcast_to((sb_bias * LOG2E)[:, :, None, None], (depth, heads, dseq, LANES)
                                 ).reshape(depth, heads * dseq, LANES)

    xp = x_prompt.reshape(bsz * seq, d)
    xs = x_sample.reshape(dbsz * dseq, d)
    outs = [[] for _ in range(6)]
    kv = None
    for l in range(depth):
        shared_o = (w_out_b, w_pg_b, w_pe_b, ln1_g3, ln1_b3, l)
        shared_f = (w_up_t, w_ffn_conv, b_conv3, w_down_b, ln2_g3, ln2_b3, l)
        z, q, *kv, vb, mix_tail = _in_proj(xp, w_in_b, w_conv_mix, l, seq_len=seq, q_dtype=BF16,
                                           kv_buffers=kv)
        o = _attn_prompt(q, kv[0], vb, sb_bias, l, head_dim=head_dim)
        x1b, r = _out_proj(z, o, xp, pp, *shared_o, alpha=alpha)
        xp, ffn_tail = _ffn(x1b, r, *shared_f, seq_len=seq)
        outs[0].append(_last_rows(mix_tail, bsz, conv_k - 1))
        outs[1].append(_last_rows(ffn_tail, bsz, conv_k - 1))
        z, q, k, v, mix_tail = _in_proj(xs, w_in_b, w_conv_mix, l, seq_len=dseq, q_dtype=F32,
                                        state=_state_rows(state_conv_mix[l], dseq))
        o = _attn_sample(q, k, v, cache_kt, cache_vt, page_table, bias_rows[l], l,
                         seq_len=dseq, head_dim=head_dim)
        x1b, r = _out_proj(z, o, xs, ps, *shared_o, alpha=alpha)
        xs, ffn_tail = _ffn(x1b, r, *shared_f, seq_len=dseq, state=_state_rows(state_conv_ffn[l], dseq))
        outs[2].append(k.reshape(dbsz, dseq, heads, head_dim))
        outs[3].append(v.reshape(dbsz, dseq, heads, head_dim))
        outs[4].append(mix_tail.reshape(dbsz, dseq, -1)[:, dseq - (conv_k - 1):])
        outs[5].append(ffn_tail.reshape(dbsz, dseq, -1)[:, dseq - (conv_k - 1):])
    kv_prompt = [a.reshape(depth, bsz, heads, head_dim, seq).transpose(0, 1, 4, 2, 3) for a in kv]
    stacked = [jnp.stack(o) for o in outs]
    return (xp.reshape(bsz, seq, d), xs.reshape(dbsz, dseq, d), *kv_prompt, *stacked)
```

```python
import functools

import jax
import jax.numpy as jnp
from jax import lax
from jax.experimental import pallas as pl
from jax.experimental.pallas import tpu as pltpu

BF16 = jnp.bfloat16
F32 = jnp.float32

LN_EPS = 1e-5
LANES = 128
SUBLANES = 8
VMEM_LIMIT = 56 * 1024 * 1024

IN_ROWS = 512
IN_COLS = 512
ATTN_BLOCK = 256
ATTN_WIDTH = 8 * LANES
PAGE_GROUP = 16
OUT_ROWS = 512
OUT_ROW_CHUNK = 256
FFN_COLS = 256
FFN_ROWS = 1024
FFN_ROW_CHUNK = 512

_dot = functools.partial(jnp.dot, preferred_element_type=F32)


def _dot_nt(a, b):
    return lax.dot_general(a, b, (((1,), (1,)), ((), ())), preferred_element_type=F32)


def _params(n_grid):
    return pltpu.CompilerParams(dimension_semantics=("arbitrary",) * n_grid,
                                vmem_limit_bytes=VMEM_LIMIT)


def _shift_rows_carry(u, prev8, shift):
    p = pltpu.roll(u, shift, axis=0)
    row8 = lax.broadcasted_iota(jnp.int32, prev8.shape, 0)
    top = jnp.where(row8 < shift, pltpu.roll(prev8, shift, axis=0), p[0:SUBLANES])
    return jnp.concatenate([top, p[SUBLANES:]], axis=0)


def _conv3_carry(u, prev8, w):
    return (w[0:1] * _shift_rows_carry(u, prev8, 2) + w[1:2] * _shift_rows_carry(u, prev8, 1)
            + w[2:3] * u)


def _conv3_state(u, e, w, seq_len):
    tm = u.shape[0]
    pos = lax.broadcasted_iota(jnp.int32, u.shape, 0) % seq_len
    p1 = jnp.where(pos >= 1, pltpu.roll(u, 1, axis=0), pltpu.roll(e, tm - 1, axis=0))
    p2 = jnp.where(pos >= 2, pltpu.roll(u, 2, axis=0), e)
    return w[0:1] * p2 + w[1:2] * p1 + w[2:3] * u


def _layer_norm(y, g, b):
    mu = jnp.mean(y, axis=-1, keepdims=True)
    d = y - mu
    var = jnp.mean(d * d, axis=-1, keepdims=True)
    return d * lax.rsqrt(var + LN_EPS) * g + b


def _gelu_tanh(x):
    return x * (0.5 * (1.0 + jnp.tanh(0.7978845608028654 * (x + 0.044715 * (x * x * x)))))


LOG2E = 1.4426950408889634


def _log2_1m_beta(z2):
    nz = -z2
    return jnp.minimum(nz, 0.0) - jnp.log(1.0 + jnp.exp2(jnp.minimum(z2, nz))) * LOG2E


def _suffix_sums(lsn, tri):
    return _dot(lsn.astype(BF16), tri)


def _tri(n):
    r = lax.broadcasted_iota(jnp.int32, (n, n), 0)
    c = lax.broadcasted_iota(jnp.int32, (n, n), 1)
    return jnp.where(r >= c, 1.0, 0.0).astype(BF16)


def _in_proj_kernel(*refs, seq_len, carry_mode, n_aliased):
    if carry_mode:
        refs = refs[:8] + refs[8 + n_aliased:]
        (x_ref, wb_ref, wc_ref, wx_ref, wq_ref, wk_ref, wv_ref, wconv_ref,
         z_ref, q_ref, kt_ref, vt_ref, vb_ref, tail_ref, carry_ref) = refs
    else:
        (x_ref, wb_ref, wc_ref, wx_ref, wq_ref, wk_ref, wv_ref, wconv_ref, e_ref,
         z_ref, q_ref, k_ref, v_ref, tail_ref) = refs
    tm = x_ref.shape[0]
    xb = x_ref[...].astype(BF16)
    u = _dot(xb, wc_ref[...]) * _dot(xb, wx_ref[...])
    if carry_mode:
        i = pl.program_id(1)

        @pl.when(i % (seq_len // tm) == 0)
        def _():
            carry_ref[...] = jnp.zeros_like(carry_ref)

        y = _conv3_carry(u, carry_ref[...], wconv_ref[...])
        carry_ref[...] = u[tm - SUBLANES:, :]
        tail_ref[...] = u[tm - SUBLANES:, :]
    else:
        y = _conv3_state(u, e_ref[...], wconv_ref[...], seq_len)
        tail_ref[...] = u
    z_ref[...] = (_dot(xb, wb_ref[...]) * y).astype(z_ref.dtype)
    q_ref[...] = _dot(xb, wq_ref[...]).astype(q_ref.dtype)
    k = _dot(xb, wk_ref[...])
    v = _dot(xb, wv_ref[...])
    if carry_mode:
        kt_ref[...] = k.T
        vt_ref[...] = v.T
        vb_ref[...] = v.astype(BF16)
    else:
        k_ref[...] = k
        v_ref[...] = v


def _in_proj(x, w_in_b, w_conv, layer, *, seq_len, q_dtype, state=None, kv_buffers=None):
    m, d = x.shape
    c = w_conv.shape[-1]
    carry_mode = state is None
    tm = min(IN_ROWS, m)
    tn = IN_COLS
    nb = c // tn
    batch = m // seq_len
    grid = (nb, m // tm)

    def wspec(s):
        return pl.BlockSpec((None, d, tn), lambda j, i, s=s: (layer, 0, s * nb + j))

    in_specs = [pl.BlockSpec((tm, d), lambda j, i: (i, 0))]
    in_specs += [wspec(s) for s in range(6)]
    in_specs += [pl.BlockSpec((None, w_conv.shape[1], tn), lambda j, i: (layer, 0, j))]
    args = [x] + [w_in_b] * 6 + [w_conv]
    tile = pl.BlockSpec((tm, tn), lambda j, i: (i, j))
    rows = lambda dt: jax.ShapeDtypeStruct((m, c), dt)
    scratch = []
    aliases = {}
    if carry_mode:
        assert seq_len % tm == 0
        per_seq = seq_len // tm
        tspec = pl.BlockSpec((None, None, tn, tm), lambda j, i: (layer, i // per_seq, j, i % per_seq))
        tshape = jax.ShapeDtypeStruct((w_in_b.shape[0], batch, c, seq_len), F32)
        if kv_buffers is not None:
            aliases = {len(args): 2, len(args) + 1: 3}
            in_specs += [pl.BlockSpec(memory_space=pl.ANY)] * 2
            args += list(kv_buffers)
        tail_shape = jax.ShapeDtypeStruct((m // tm, SUBLANES, c), F32)
        tail_spec = pl.BlockSpec((None, SUBLANES, tn), lambda j, i: (i, 0, j))
        scratch = [pltpu.VMEM((SUBLANES, tn), F32)]
        out_shape = (rows(BF16), rows(q_dtype), tshape, tshape, rows(BF16), tail_shape)
        out_specs = (tile, tile, tspec, tspec, tile, tail_spec)
    else:
        assert tm == m and seq_len == SUBLANES
        in_specs.append(tile)
        args.append(state)
        out_shape = (rows(BF16), rows(q_dtype), rows(F32), rows(F32), rows(F32))
        out_specs = (tile,) * 5
    return pl.pallas_call(
        functools.partial(_in_proj_kernel, seq_len=seq_len, carry_mode=carry_mode, n_aliased=len(aliases)),
        grid=grid, in_specs=in_specs, out_specs=out_specs, input_output_aliases=aliases,
        out_shape=out_shape, scratch_shapes=scratch, compiler_params=_params(2),
        name="in_proj")(*args)


def _attn_prompt_kernel(bias_ref, q_ref, kt_ref, v_ref, o_ref,
                        kb_ref, tri_ref, acc_ref, car_ref, s_ref, a_ref, *, layer, tk, head_dim, scale):
    c = pl.program_id(1)
    qi = pl.program_id(2)
    tq, width = q_ref.shape
    heads_per_tile = LANES // head_dim
    n_heads = width // head_dim
    nk = kt_ref.shape[1] // tk
    hs = range(n_heads)
    tile = lambda h: slice((h // heads_per_tile) * LANES, (h // heads_per_tile + 1) * LANES)

    @pl.when(qi == 0)
    def _():
        chan = lax.broadcasted_iota(jnp.int32, (LANES, tk), 0)
        for j in range(nk):
            for h in hs:
                lo = (h % heads_per_tile) * head_dim
                kt = kt_ref[tile(h), j * tk:(j + 1) * tk] * (scale * LOG2E)
                kb_ref[j, h] = jnp.where((chan >= lo) & (chan < lo + head_dim), kt, 0.0).astype(BF16)
        tri_ref[...] = _tri(tk)

    acc_ref[...] = jnp.zeros_like(acc_ref)
    qs = [q_ref[:, tile(h)] for h in hs]
    bias2 = [bias_ref[layer, c * n_heads + h] * LOG2E for h in hs]

    def logit(j, h):
        return _dot(qs[h], kb_ref[j, h]) + bias2[h]

    tri = tri_ref[...]
    jn = jnp.maximum(qi - 1, 0)
    row = lax.broadcasted_iota(jnp.int32, (tq, tk), 0)
    col = lax.broadcasted_iota(jnp.int32, (tq, tk), 1)
    valid = col < row
    z_diag, z_next, locs = [logit(qi, 0)], [], []
    for h in hs:
        if h + 1 < n_heads:
            z_diag.append(logit(qi, h + 1))
        lsn = jnp.where(valid, _log2_1m_beta(z_diag[h]), 0.0)
        z_next.append(logit(jn, h))
        locs.append(_suffix_sums(lsn, tri))
    for h in hs:
        a = jnp.where(valid, jnp.exp2(z_diag[h] + locs[h]), 0.0)
        car_ref[h] = jnp.broadcast_to(locs[h][:, 0:1], (tq, LANES))
        a_ref[h] = a.astype(BF16)
        s_ref[h] = z_next[h]

    def body(it, j_prev):
        j = qi - 1 - it
        jn = jnp.maximum(j - 1, 0)
        koff = pl.multiple_of(j_prev * tk, tk)
        tri = tri_ref[...]
        n = len(hs)
        locs = [None] * n

        def apply_prev(h):
            acc_ref[h] += _dot(a_ref[h], v_ref[pl.ds(koff, tk), tile(h)])

        def finish(h):
            carry = car_ref[h]
            a = jnp.exp2(s_ref[h] + locs[h] + jnp.tile(carry, (1, tk // LANES)))
            car_ref[h] = carry + jnp.broadcast_to(locs[h][:, 0:1], carry.shape)
            a_ref[h] = a.astype(BF16)

        apply_prev(0)
        for h in hs:
            lsn = _log2_1m_beta(s_ref[h])
            if h + 1 < n:
                apply_prev(h + 1)
            locs[h] = _suffix_sums(lsn, tri)
        for h in hs:
            z_next = logit(jn, h)
            finish(h)
            s_ref[h] = z_next
        return j

    j_last = lax.fori_loop(0, qi, body, qi)
    koff = pl.multiple_of(j_last * tk, tk)
    outs = [_dot(a_ref[h], v_ref[pl.ds(koff, tk), tile(h)]) for h in hs]

    lane = lax.broadcasted_iota(jnp.int32, (tq, LANES), 1)
    totals = [acc_ref[h] + outs[h] for h in hs]
    for t in range(width // LANES):
        out = totals[t * heads_per_tile]
        for h in range(1, heads_per_tile):
            out = jnp.where(lane >= h * head_dim, totals[t * heads_per_tile + h], out)
        o_ref[:, t * LANES:(t + 1) * LANES] = out.astype(o_ref.dtype)


def _attn_prompt(q, kt, v, sb_bias, layer, *, head_dim):
    m, w = q.shape
    _, batch, _, seq_len = kt.shape
    tq = tk = ATTN_BLOCK
    width = min(ATTN_WIDTH, w)
    n_heads = width // head_dim
    nq = seq_len // tq
    grid = (batch, w // width, nq)
    qspec = pl.BlockSpec((tq, width), lambda b, c, i: (b * nq + i, c))
    return pl.pallas_call(
        functools.partial(_attn_prompt_kernel, layer=layer, tk=tk, head_dim=head_dim,
                          scale=head_dim ** -0.5),
        grid=grid,
        in_specs=[pl.BlockSpec(memory_space=pltpu.SMEM), qspec,
                  pl.BlockSpec((None, None, width, seq_len), lambda b, c, i: (layer, b, c, 0)),
                  pl.BlockSpec((seq_len, width), lambda b, c, i: (b, c))],
        out_specs=qspec,
        out_shape=jax.ShapeDtypeStruct((m, w), BF16),
        scratch_shapes=[pltpu.VMEM((seq_len // tk, n_heads, LANES, tk), BF16),
                        pltpu.VMEM((tk, tk), BF16),
                        pltpu.VMEM((n_heads, tq, LANES), F32),
                        pltpu.VMEM((n_heads, tq, LANES), F32),
                        pltpu.VMEM((n_heads, tq, tk), F32),
                        pltpu.VMEM((n_heads, tq, tk), BF16)],
        compiler_params=_params(3), name="attn_prompt")(sb_bias, q, kt, v)


def _attn_sample_kernel(pt_ref, bias_ref, q_ref, kn_ref, vn_ref, *refs, heads, head_dim, scale, group):
    del pt_ref
    ck_refs, cv_refs = refs[:group], refs[group:2 * group]
    o_ref, wt_ref, tri_ref, acc_ref, car_ref = refs[2 * group:]
    j = pl.program_id(1)
    t, w = q_ref.shape
    page = ck_refs[0].shape[1]
    rows = heads * t

    def process(ks, vs, new_rows):
        wt, tri, bias2 = wt_ref[...], tri_ref[...], bias_ref[...]
        qk = _dot_nt if new_rows else _dot
        zs = [qk(wt, k.astype(BF16)) * (scale * LOG2E) + bias2 for k in ks]
        lsns = [_log2_1m_beta(z) for z in zs]
        if new_rows:
            row = lax.broadcasted_iota(jnp.int32, (rows, page), 0)
            col = lax.broadcasted_iota(jnp.int32, (rows, page), 1)
            valid = col < row % t
            lsns = [jnp.where(valid, lsn, 0.0) for lsn in lsns]
        locs = [_suffix_sums(lsn, tri) for lsn in lsns]
        carry = car_ref[...]
        avs = []
        for z, loc in zip(zs, locs):
            avs.append(jnp.exp2(z + loc + carry))
            carry = carry + jnp.broadcast_to(loc[:, 0:1], carry.shape)
        car_ref[...] = carry
        if new_rows:
            avs = [jnp.where(valid, a, 0.0) for a in avs]
        a_all = jnp.concatenate([a.astype(BF16) for a in avs], axis=1)
        if new_rows:
            acc_ref[...] += _dot(a_all, jnp.concatenate([v.astype(BF16) for v in vs], axis=0))
        else:
            acc_ref[...] += _dot_nt(a_all, jnp.concatenate([v.astype(BF16) for v in vs], axis=1))

    @pl.when(j == 0)
    def _():
        qt = jnp.tile(q_ref[...], (heads, 1))
        rowh = lax.broadcasted_iota(jnp.int32, (rows, w), 0) // t
        colh = lax.broadcasted_iota(jnp.int32, (rows, w), 1) // head_dim
        wt_ref[...] = jnp.where(rowh == colh, qt, 0.0).astype(BF16)
        tri_ref[...] = _tri(page)
        acc_ref[...] = jnp.zeros_like(acc_ref)
        car_ref[...] = jnp.zeros_like(car_ref)
        pad = jnp.zeros((page - t, w), F32)
        process([jnp.concatenate([kn_ref[...], pad], axis=0)],
                [jnp.concatenate([vn_ref[...], pad], axis=0)], True)

    @pl.when(j > 0)
    def _():
        process([r[...] for r in ck_refs], [r[...] for r in cv_refs], False)

    @pl.when(j == pl.num_programs(1) - 1)
    def _():
        heads_per_tile = LANES // head_dim
        lane = lax.broadcasted_iota(jnp.int32, (t, LANES), 1)
        for p in range(w // LANES):
            cols = slice(p * LANES, (p + 1) * LANES)
            r0 = p * heads_per_tile * t
            out = acc_ref[r0:r0 + t, cols]
            for h in range(1, heads_per_tile):
                out = jnp.where(lane >= h * head_dim, acc_ref[r0 + h * t:r0 + (h + 1) * t, cols], out)
            o_ref[:, cols] = out


def _attn_sample(q, k_new, v_new, cache_kt, cache_vt, page_table, bias_rows, layer, *, seq_len, head_dim):
    m, w = q.shape
    dbsz, n_pages = page_table.shape
    page = cache_kt.shape[3]
    heads = w // head_dim
    rows = heads * seq_len
    group = min(PAGE_GROUP, n_pages)
    assert page == LANES and seq_len == SUBLANES and n_pages % group == 0
    grid = (dbsz, n_pages // group + 1)
    rowspec = pl.BlockSpec((seq_len, w), lambda b, j, pt: (b, 0))

    def pagespec(g):
        return pl.BlockSpec(
            (None, None, w, page),
            lambda b, j, pt: (layer, pt[b, n_pages - 1 - ((jnp.maximum(j, 1) - 1) * group + g)], 0, 0))

    pagespecs = [pagespec(g) for g in range(group)]
    grid_spec = pltpu.PrefetchScalarGridSpec(
        num_scalar_prefetch=1, grid=grid,
        in_specs=[pl.BlockSpec((rows, LANES), lambda b, j, pt: (0, 0)),
                  rowspec, rowspec, rowspec] + pagespecs + pagespecs,
        out_specs=rowspec,
        scratch_shapes=[pltpu.VMEM((rows, w), BF16),
                        pltpu.VMEM((page, page), BF16),
                        pltpu.VMEM((rows, w), F32),
                        pltpu.VMEM((rows, LANES), F32)])
    return pl.pallas_call(
        functools.partial(_attn_sample_kernel, heads=heads, head_dim=head_dim, scale=head_dim ** -0.5,
                          group=group),
        grid_spec=grid_spec, out_shape=jax.ShapeDtypeStruct((m, w), F32),
        compiler_params=_params(2), name="attn_sample")(
            page_table, bias_rows, q, k_new, v_new, *([cache_kt] * group), *([cache_vt] * group))


def _out_proj_kernel(z_ref, o_ref, x_ref, p_ref, wout_ref, wpg_ref, wpe_ref, g_ref, b_ref,
                     x1b_ref, r_ref, *, alpha, row_chunk):
    chunks = [slice(r0, r0 + row_chunk) for r0 in range(0, x_ref.shape[0], row_chunk)]
    g, b = g_ref[...], b_ref[...]
    mixed = [jnp.concatenate([z_ref[rows, :].astype(BF16), o_ref[rows, :].astype(BF16)], axis=1)
             for rows in chunks]
    ys = [alpha * x_ref[rows, :] + _dot(mx, wout_ref[...]) for rows, mx in zip(chunks, mixed)]
    x1s = [_layer_norm(y, g, b) for y in ys]
    x1bs = [x1.astype(BF16) for x1 in x1s]
    logits = [_dot(x1b, wpg_ref[...]) for x1b in x1bs]
    pes = [_dot(p_ref[rows, :].astype(BF16), wpe_ref[...]) for rows in chunks]
    for rows, x1, x1b, lg, pe in zip(chunks, x1s, x1bs, logits, pes):
        x1b_ref[rows, :] = x1b
        r_ref[rows, :] = alpha * x1 + pe * (1.0 / (1.0 + jnp.exp(-lg)))


def _out_proj(z, o, x, p, w_out_b, w_pg_b, w_pe_b, ln_g, ln_b, layer, *, alpha):
    m, d = x.shape
    c = z.shape[1]
    pdim = p.shape[-1]
    tm = min(OUT_ROWS, m)
    rowspec = lambda width: pl.BlockSpec((tm, width), lambda i: (i, 0))
    const = lambda r, cc: pl.BlockSpec((None, r, cc), lambda i: (layer, 0, 0),
                                       pipeline_mode=pl.Buffered(1))
    return pl.pallas_call(
        functools.partial(_out_proj_kernel, alpha=alpha, row_chunk=min(OUT_ROW_CHUNK, tm)),
        grid=(m // tm,),
        in_specs=[rowspec(c), rowspec(c), rowspec(d),
                  pl.BlockSpec((None, tm, pdim), lambda i: (layer, i, 0)),
                  const(2 * c, d), const(d, d), const(pdim, d), const(1, d), const(1, d)],
        out_specs=(rowspec(d), rowspec(d)),
        out_shape=(jax.ShapeDtypeStruct((m, d), BF16), jax.ShapeDtypeStruct((m, d), F32)),
        compiler_params=_params(1), name="out_proj")(
            z, o, x, p, w_out_b, w_pg_b, w_pe_b, ln_g, ln_b)


def _ffn_kernel(*refs, seq_len, carry_mode, row_chunk):
    if carry_mode:
        (x1b_ref, r_ref, wa_ref, wg_ref, wconv_ref, bconv_ref, wd_ref, g_ref, b_ref,
         out_ref, tail_ref, carry_ref) = refs
    else:
        (x1b_ref, r_ref, wa_ref, wg_ref, wconv_ref, bconv_ref, wd_ref, g_ref, b_ref, e_ref,
         out_ref, tail_ref) = refs
    i = pl.program_id(0)
    f = pl.program_id(1)
    tm = x1b_ref.shape[0]

    @pl.when(f == 0)
    def _():
        out_ref[...] = r_ref[...]

    if carry_mode:
        @pl.when(i % (seq_len // tm) == 0)
        def _():
            carry_ref[f] = jnp.zeros(carry_ref.shape[1:], F32)

        prev8 = carry_ref[f]
    chunks = [slice(r0, r0 + row_chunk) for r0 in range(0, tm, row_chunk)]
    ups = [(_dot(x1b_ref[rows, :], wa_ref[...]), _dot(x1b_ref[rows, :], wg_ref[...])) for rows in chunks]
    hs = []
    for rows, (a, gate) in zip(chunks, ups):
        if carry_mode:
            a_c = _conv3_carry(a, prev8, wconv_ref[...])
            prev8 = a[row_chunk - SUBLANES:, :]
        else:
            a_c = _conv3_state(a, e_ref[rows, :], wconv_ref[...], seq_len)
            tail_ref[rows, :] = a
        hs.append((_gelu_tanh(a_c + bconv_ref[...]) * gate).astype(BF16))
    for rows, h in zip(chunks, hs):
        out_ref[rows, :] += _dot(h, wd_ref[...])
    if carry_mode:
        carry_ref[f] = prev8
        tail_ref[...] = prev8

    @pl.when(f == pl.num_programs(1) - 1)
    def _():
        out_ref[...] = _layer_norm(out_ref[...], g_ref[...], b_ref[...])


def _ffn(x1b, r, w_up_b, w_conv, b_conv, w_down_b, ln_g, ln_b, layer, *, seq_len, state=None):
    m, d = r.shape
    dff = w_conv.shape[-1]
    carry_mode = state is None
    tf = FFN_COLS
    nf = dff // tf
    tm = min(FFN_ROWS, m)
    assert dff % tf == 0
    rowspec = pl.BlockSpec((tm, d), lambda i, f: (i, 0))
    vec = lambda rows, width: pl.BlockSpec((None, rows, width), lambda i, f: (layer, 0, 0))
    in_specs = [rowspec, rowspec,
                pl.BlockSpec((None, d, tf), lambda i, f: (layer, 0, f)),
                pl.BlockSpec((None, d, tf), lambda i, f: (layer, 0, nf + f)),
                pl.BlockSpec((None, w_conv.shape[1], tf), lambda i, f: (layer, 0, f)),
                pl.BlockSpec((None, 1, tf), lambda i, f: (layer, 0, f)),
                pl.BlockSpec((None, tf, d), lambda i, f: (layer, f, 0)),
                vec(1, d), vec(1, d)]
    args = [x1b, r, w_up_b, w_up_b, w_conv, b_conv, w_down_b, ln_g, ln_b]
    scratch = []
    if carry_mode:
        assert seq_len % tm == 0
        tail_shape = jax.ShapeDtypeStruct((m // tm, SUBLANES, dff), F32)
        tail_spec = pl.BlockSpec((None, SUBLANES, tf), lambda i, f: (i, 0, f))
        scratch = [pltpu.VMEM((nf, SUBLANES, tf), F32)]
    else:
        assert tm == m and seq_len == SUBLANES
        tile = pl.BlockSpec((tm, tf), lambda i, f: (i, f))
        in_specs.append(tile)
        args.append(state)
        tail_shape = jax.ShapeDtypeStruct((m, dff), F32)
        tail_spec = tile
    return pl.pallas_call(
        functools.partial(_ffn_kernel, seq_len=seq_len, carry_mode=carry_mode,
                          row_chunk=min(FFN_ROW_CHUNK, tm)),
        grid=(m // tm, nf), in_specs=in_specs, out_specs=(rowspec, tail_spec),
        out_shape=(jax.ShapeDtypeStruct((m, d), F32), tail_shape),
        scratch_shapes=scratch, compiler_params=_params(2), name="ffn")(*args)


def _state_rows(state, seq_len):
    b, k1, c = state.shape
    return jnp.pad(state, ((0, 0), (0, seq_len - k1), (0, 0))).reshape(b * seq_len, c)


def _last_rows(tails, batch, n):
    per_seq = tails.shape[0] // batch
    return tails[per_seq - 1::per_seq, SUBLANES - n:]


def kernel(x_prompt, x_sample, cache_k, cache_v, state_conv_mix, state_conv_ffn, page_table, p_prompt,
           p_sample, w_in, w_conv_mix, sb_bias, w_out, ln1_g, ln1_b, w_up, w_ffn_conv, b_ffn_conv,
           w_down, w_pg, w_pe, ln2_g, ln2_b):
    bsz, seq, d = x_prompt.shape
    dbsz, dseq, _ = x_sample.shape
    depth, n_pool, page, heads, head_dim = cache_k.shape
    width = heads * head_dim
    conv_k = w_conv_mix.shape[1]
    assert conv_k == 3 and w_ffn_conv.shape[1] == 3
    alpha = (2.0 * depth) ** 0.25

    w_in_b, w_out_b, w_up_b = w_in.astype(BF16), w_out.astype(BF16), w_up.astype(BF16)
    w_down_b, w_pg_b, w_pe_b = w_down.astype(BF16), w_pg.astype(BF16), w_pe.astype(BF16)
    vec3 = lambda a: a.reshape(depth, 1, a.shape[-1])
    ln1_g3, ln1_b3, ln2_g3, ln2_b3, b_conv3 = map(vec3, (ln1_g, ln1_b, ln2_g, ln2_b, b_ffn_conv))
    cache_kt = cache_k.transpose(0, 1, 3, 4, 2).reshape(depth, n_pool, width, page)
    cache_vt = cache_v.transpose(0, 1, 3, 4, 2).reshape(depth, n_pool, width, page)
    pp = p_prompt.reshape(depth, bsz * seq, -1)
    ps = p_sample.reshape(depth, dbsz * dseq, -1)
    bias_rows = jnp.broadcast_to((sb_bias * LOG2E)[:, :, None, None], (depth, heads, dseq, LANES)
                                 ).reshape(depth, heads * dseq, LANES)

    xp = x_prompt.reshape(bsz * seq, d)
    xs = x_sample.reshape(dbsz * dseq, d)
    outs = [[] for _ in range(6)]
    kv = None
    for l in range(depth):
        shared_o = (w_out_b, w_pg_b, w_pe_b, ln1_g3, ln1_b3, l)
        shared_f = (w_up_b, w_ffn_conv, b_conv3, w_down_b, ln2_g3, ln2_b3, l)
        z, q, *kv, vb, mix_tail = _in_proj(xp, w_in_b, w_conv_mix, l, seq_len=seq, q_dtype=BF16,
                                           kv_buffers=kv)
        o = _attn_prompt(q, kv[0], vb, sb_bias, l, head_dim=head_dim)
        x1b, r = _out_proj(z, o, xp, pp, *shared_o, alpha=alpha)
        xp, ffn_tail = _ffn(x1b, r, *shared_f, seq_len=seq)
        outs[0].append(_last_rows(mix_tail, bsz, conv_k - 1))
        outs[1].append(_last_rows(ffn_tail, bsz, conv_k - 1))
        z, q, k, v, mix_tail = _in_proj(xs, w_in_b, w_conv_mix, l, seq_len=dseq, q_dtype=F32,
                                        state=_state_rows(state_conv_mix[l], dseq))
        o = _attn_sample(q, k, v, cache_kt, cache_vt, page_table, bias_rows[l], l,
                         seq_len=dseq, head_dim=head_dim)
        x1b, r = _out_proj(z, o, xs, ps, *shared_o, alpha=alpha)
        xs, ffn_tail = _ffn(x1b, r, *shared_f, seq_len=dseq, state=_state_rows(state_conv_ffn[l], dseq))
        outs[2].append(k.reshape(dbsz, dseq, heads, head_dim))
        outs[3].append(v.reshape(dbsz, dseq, heads, head_dim))
        outs[4].append(mix_tail.reshape(dbsz, dseq, -1)[:, dseq - (conv_k - 1):])
        outs[5].append(ffn_tail.reshape(dbsz, dseq, -1)[:, dseq - (conv_k - 1):])
    kv_prompt = [a.reshape(depth, bsz, heads, head_dim, seq).transpose(0, 1, 4, 2, 3) for a in kv]
    stacked = [jnp.stack(o) for o in outs]
    return (xp.reshape(bsz, seq, d), xs.reshape(dbsz, dseq, d), *kv_prompt, *stacked)
```

```python
import functools

import jax
import jax.numpy as jnp
from jax import lax
from jax.experimental import pallas as pl
from jax.experimental.pallas import tpu as pltpu

BF16 = jnp.bfloat16
F32 = jnp.float32

LN_EPS = 1e-5
LANES = 128
SUBLANES = 8
VMEM_LIMIT = 56 * 1024 * 1024

IN_ROWS = 512
IN_COLS = 512
ATTN_BLOCK = 256
ATTN_WIDTH = 8 * LANES
PAGE_GROUP = 16
OUT_ROWS = 512
OUT_ROW_CHUNK = 256
FFN_COLS = 256
FFN_ROWS = 1024
FFN_ROW_CHUNK = 512

_dot = functools.partial(jnp.dot, preferred_element_type=F32)


def _dot_nt(a, b):
    return lax.dot_general(a, b, (((1,), (1,)), ((), ())), preferred_element_type=F32)


def _params(n_grid):
    return pltpu.CompilerParams(dimension_semantics=("arbitrary",) * n_grid,
                                vmem_limit_bytes=VMEM_LIMIT)


def _shift_rows_carry(u, prev8, shift):
    p = pltpu.roll(u, shift, axis=0)
    row8 = lax.broadcasted_iota(jnp.int32, prev8.shape, 0)
    top = jnp.where(row8 < shift, pltpu.roll(prev8, shift, axis=0), p[0:SUBLANES])
    return jnp.concatenate([top, p[SUBLANES:]], axis=0)


def _conv3_carry(u, prev8, w):
    return (w[0:1] * _shift_rows_carry(u, prev8, 2) + w[1:2] * _shift_rows_carry(u, prev8, 1)
            + w[2:3] * u)


def _conv3_state(u, e, w, seq_len):
    tm = u.shape[0]
    pos = lax.broadcasted_iota(jnp.int32, u.shape, 0) % seq_len
    p1 = jnp.where(pos >= 1, pltpu.roll(u, 1, axis=0), pltpu.roll(e, tm - 1, axis=0))
    p2 = jnp.where(pos >= 2, pltpu.roll(u, 2, axis=0), e)
    return w[0:1] * p2 + w[1:2] * p1 + w[2:3] * u


def _layer_norm(y, g, b):
    mu = jnp.mean(y, axis=-1, keepdims=True)
    d = y - mu
    var = jnp.mean(d * d, axis=-1, keepdims=True)
    return d * lax.rsqrt(var + LN_EPS) * g + b


def _gelu_tanh(x):
    return x * (0.5 * (1.0 + jnp.tanh(0.7978845608028654 * (x + 0.044715 * (x * x * x)))))


LOG2E = 1.4426950408889634


def _log2_1m_beta(z2):
    nz = -z2
    return jnp.minimum(nz, 0.0) - jnp.log(1.0 + jnp.exp2(jnp.minimum(z2, nz))) * LOG2E


def _suffix_sums(lsn, tri):
    return _dot(lsn.astype(BF16), tri)


def _tri(n):
    r = lax.broadcasted_iota(jnp.int32, (n, n), 0)
    c = lax.broadcasted_iota(jnp.int32, (n, n), 1)
    return jnp.where(r >= c, 1.0, 0.0).astype(BF16)


def _in_proj_kernel(*refs, seq_len, carry_mode, n_aliased):
    if carry_mode:
        refs = refs[:8] + refs[8 + n_aliased:]
        (x_ref, wb_ref, wc_ref, wx_ref, wq_ref, wk_ref, wv_ref, wconv_ref,
         z_ref, q_ref, kt_ref, vt_ref, vb_ref, tail_ref, carry_ref) = refs
    else:
        (x_ref, wb_ref, wc_ref, wx_ref, wq_ref, wk_ref, wv_ref, wconv_ref, e_ref,
         z_ref, q_ref, k_ref, v_ref, tail_ref) = refs
    tm = x_ref.shape[0]
    xb = x_ref[...].astype(BF16)
    u = _dot(xb, wc_ref[...]) * _dot(xb, wx_ref[...])
    if carry_mode:
        i = pl.program_id(1)

        @pl.when(i % (seq_len // tm) == 0)
        def _():
            carry_ref[...] = jnp.zeros_like(carry_ref)

        y = _conv3_carry(u, carry_ref[...], wconv_ref[...])
        carry_ref[...] = u[tm - SUBLANES:, :]
        tail_ref[...] = u[tm - SUBLANES:, :]
    else:
        y = _conv3_state(u, e_ref[...], wconv_ref[...], seq_len)
        tail_ref[...] = u
    z_ref[...] = (_dot(xb, wb_ref[...]) * y).astype(z_ref.dtype)
    q_ref[...] = _dot(xb, wq_ref[...]).astype(q_ref.dtype)
    k = _dot(xb, wk_ref[...])
    v = _dot(xb, wv_ref[...])
    if carry_mode:
        kt_ref[...] = k.T
        vt_ref[...] = v.T
        vb_ref[...] = v.astype(BF16)
    else:
        k_ref[...] = k
        v_ref[...] = v


def _in_proj(x, w_in_b, w_conv, layer, *, seq_len, q_dtype, state=None, kv_buffers=None):
    m, d = x.shape
    c = w_conv.shape[-1]
    carry_mode = state is None
    tm = min(IN_ROWS, m)
    tn = IN_COLS
    nb = c // tn
    batch = m // seq_len
    grid = (nb, m // tm)

    def wspec(s):
        return pl.BlockSpec((None, d, tn), lambda j, i, s=s: (layer, 0, s * nb + j))

    in_specs = [pl.BlockSpec((tm, d), lambda j, i: (i, 0))]
    in_specs += [wspec(s) for s in range(6)]
    in_specs += [pl.BlockSpec((None, w_conv.shape[1], tn), lambda j, i: (layer, 0, j))]
    args = [x] + [w_in_b] * 6 + [w_conv]
    tile = pl.BlockSpec((tm, tn), lambda j, i: (i, j))
    rows = lambda dt: jax.ShapeDtypeStruct((m, c), dt)
    scratch = []
    aliases = {}
    if carry_mode:
        assert seq_len % tm == 0
        per_seq = seq_len // tm
        tspec = pl.BlockSpec((None, None, tn, tm), lambda j, i: (layer, i // per_seq, j, i % per_seq))
        tshape = jax.ShapeDtypeStruct((w_in_b.shape[0], batch, c, seq_len), F32)
        if kv_buffers is not None:
            aliases = {len(args): 2, len(args) + 1: 3}
            in_specs += [pl.BlockSpec(memory_space=pl.ANY)] * 2
            args += list(kv_buffers)
        tail_shape = jax.ShapeDtypeStruct((m // tm, SUBLANES, c), F32)
        tail_spec = pl.BlockSpec((None, SUBLANES, tn), lambda j, i: (i, 0, j))
        scratch = [pltpu.VMEM((SUBLANES, tn), F32)]
        out_shape = (rows(BF16), rows(q_dtype), tshape, tshape, rows(BF16), tail_shape)
        out_specs = (tile, tile, tspec, tspec, tile, tail_spec)
    else:
        assert tm == m and seq_len == SUBLANES
        in_specs.append(tile)
        args.append(state)
        out_shape = (rows(BF16), rows(q_dtype), rows(F32), rows(F32), rows(F32))
        out_specs = (tile,) * 5
    return pl.pallas_call(
        functools.partial(_in_proj_kernel, seq_len=seq_len, carry_mode=carry_mode, n_aliased=len(aliases)),
        grid=grid, in_specs=in_specs, out_specs=out_specs, input_output_aliases=aliases,
        out_shape=out_shape, scratch_shapes=scratch, compiler_params=_params(2),
        name="in_proj")(*args)


def _attn_prompt_kernel(bias_ref, q_ref, kt_ref, v_ref, o_ref,
                        kb_ref, tri_ref, acc_ref, car_ref, s_ref, a_ref, *, layer, tk, head_dim, scale):
    c = pl.program_id(1)
    qi = pl.program_id(2)
    tq, width = q_ref.shape
    heads_per_tile = LANES // head_dim
    n_heads = width // head_dim
    nk = kt_ref.shape[1] // tk
    hs = range(n_heads)
    tile = lambda h: slice((h // heads_per_tile) * LANES, (h // heads_per_tile + 1) * LANES)

    @pl.when(qi == 0)
    def _():
        chan = lax.broadcasted_iota(jnp.int32, (LANES, tk), 0)
        for j in range(nk):
            for h in hs:
                lo = (h % heads_per_tile) * head_dim
                kt = kt_ref[tile(h), j * tk:(j + 1) * tk] * (scale * LOG2E)
                kb_ref[j, h] = jnp.where((chan >= lo) & (chan < lo + head_dim), kt, 0.0).astype(BF16)
        tri_ref[...] = _tri(tk)

    acc_ref[...] = jnp.zeros_like(acc_ref)
    qs = [q_ref[:, tile(h)] for h in hs]
    bias2 = [bias_ref[layer, c * n_heads + h] * LOG2E for h in hs]

    def logit(j, h):
        return _dot(qs[h], kb_ref[j, h]) + bias2[h]

    tri = tri_ref[...]
    jn = jnp.maximum(qi - 1, 0)
    row = lax.broadcasted_iota(jnp.int32, (tq, tk), 0)
    col = lax.broadcasted_iota(jnp.int32, (tq, tk), 1)
    valid = col < row
    z_diag, z_next, locs = [logit(qi, 0)], [], []
    for h in hs:
        if h + 1 < n_heads:
            z_diag.append(logit(qi, h + 1))
        lsn = jnp.where(valid, _log2_1m_beta(z_diag[h]), 0.0)
        z_next.append(logit(jn, h))
        locs.append(_suffix_sums(lsn, tri))
    for h in hs:
        a = jnp.where(valid, jnp.exp2(z_diag[h] + locs[h]), 0.0)
        car_ref[h] = jnp.broadcast_to(locs[h][:, 0:1], (tq, LANES))
        a_ref[h] = a.astype(BF16)
        s_ref[h] = z_next[h]

    def body(it, j_prev):
        j = qi - 1 - it
        jn = jnp.maximum(j - 1, 0)
        koff = pl.multiple_of(j_prev * tk, tk)
        tri = tri_ref[...]
        n = len(hs)
        locs = [None] * n

        def apply_prev(h):
            acc_ref[h] += _dot(a_ref[h], v_ref[pl.ds(koff, tk), tile(h)])

        def finish(h):
            carry = car_ref[h]
            a = jnp.exp2(s_ref[h] + locs[h] + jnp.tile(carry, (1, tk // LANES)))
            car_ref[h] = carry + jnp.broadcast_to(locs[h][:, 0:1], carry.shape)
            a_ref[h] = a.astype(BF16)

        apply_prev(0)
        for h in hs:
            lsn = _log2_1m_beta(s_ref[h])
            if h + 1 < n:
                apply_prev(h + 1)
            locs[h] = _suffix_sums(lsn, tri)
        for h in hs:
            z_next = logit(jn, h)
            finish(h)
            s_ref[h] = z_next
        return j

    j_last = lax.fori_loop(0, qi, body, qi)
    koff = pl.multiple_of(j_last * tk, tk)
    outs = [_dot(a_ref[h], v_ref[pl.ds(koff, tk), tile(h)]) for h in hs]

    lane = lax.broadcasted_iota(jnp.int32, (tq, LANES), 1)
    totals = [acc_ref[h] + outs[h] for h in hs]
    for t in range(width // LANES):
        out = totals[t * heads_per_tile]
        for h in range(1, heads_per_tile):
            out = jnp.where(lane >= h * head_dim, totals[t * heads_per_tile + h], out)
        o_ref[:, t * LANES:(t + 1) * LANES] = out.astype(o_ref.dtype)


def _attn_prompt(q, kt, v, sb_bias, layer, *, head_dim):
    m, w = q.shape
    _, batch, _, seq_len = kt.shape
    tq = tk = ATTN_BLOCK
    width = min(ATTN_WIDTH, w)
    n_heads = width // head_dim
    nq = seq_len // tq
    grid = (batch, w // width, nq)
    qspec = pl.BlockSpec((tq, width), lambda b, c, i: (b * nq + i, c))
    return pl.pallas_call(
        functools.partial(_attn_prompt_kernel, layer=layer, tk=tk, head_dim=head_dim,
                          scale=head_dim ** -0.5),
        grid=grid,
        in_specs=[pl.BlockSpec(memory_space=pltpu.SMEM), qspec,
                  pl.BlockSpec((None, None, width, seq_len), lambda b, c, i: (layer, b, c, 0)),
                  pl.BlockSpec((seq_len, width), lambda b, c, i: (b, c))],
        out_specs=qspec,
        out_shape=jax.ShapeDtypeStruct((m, w), BF16),
        scratch_shapes=[pltpu.VMEM((seq_len // tk, n_heads, LANES, tk), BF16),
                        pltpu.VMEM((tk, tk), BF16),
                        pltpu.VMEM((n_heads, tq, LANES), F32),
                        pltpu.VMEM((n_heads, tq, LANES), F32),
                        pltpu.VMEM((n_heads, tq, tk), F32),
                        pltpu.VMEM((n_heads, tq, tk), BF16)],
        compiler_params=_params(3), name="attn_prompt")(sb_bias, q, kt, v)


def _attn_sample_kernel(pt_ref, bias_ref, q_ref, kn_ref, vn_ref, *refs, heads, head_dim, scale, group):
    del pt_ref
    ck_refs, cv_refs = refs[:group], refs[group:2 * group]
    o_ref, wt_ref, tri_ref, acc_ref, car_ref = refs[2 * group:]
    j = pl.program_id(1)
    t, w = q_ref.shape
    page = ck_refs[0].shape[1]
    rows = heads * t

    def process(ks, vs, new_rows):
        wt, tri, bias2 = wt_ref[...], tri_ref[...], bias_ref[...]
        qk = _dot_nt if new_rows else _dot
        zs = [qk(wt, k.astype(BF16)) * (scale * LOG2E) + bias2 for k in ks]
        lsns = [_log2_1m_beta(z) for z in zs]
        if new_rows:
            row = lax.broadcasted_iota(jnp.int32, (rows, page), 0)
            col = lax.broadcasted_iota(jnp.int32, (rows, page), 1)
            valid = col < row % t
            lsns = [jnp.where(valid, lsn, 0.0) for lsn in lsns]
        locs = [_suffix_sums(lsn, tri) for lsn in lsns]
        carry = car_ref[...]
        avs = []
        for z, loc in zip(zs, locs):
            avs.append(jnp.exp2(z + loc + carry))
            carry = carry + jnp.broadcast_to(loc[:, 0:1], carry.shape)
        car_ref[...] = carry
        if new_rows:
            avs = [jnp.where(valid, a, 0.0) for a in avs]
        a_all = jnp.concatenate([a.astype(BF16) for a in avs], axis=1)
        if new_rows:
            acc_ref[...] += _dot(a_all, jnp.concatenate([v.astype(BF16) for v in vs], axis=0))
        else:
            acc_ref[...] += _dot_nt(a_all, jnp.concatenate([v.astype(BF16) for v in vs], axis=1))

    @pl.when(j == 0)
    def _():
        qt = jnp.tile(q_ref[...], (heads, 1))
        rowh = lax.broadcasted_iota(jnp.int32, (rows, w), 0) // t
        colh = lax.broadcasted_iota(jnp.int32, (rows, w), 1) // head_dim
        wt_ref[...] = jnp.where(rowh == colh, qt, 0.0).astype(BF16)
        tri_ref[...] = _tri(page)
        acc_ref[...] = jnp.zeros_like(acc_ref)
        car_ref[...] = jnp.zeros_like(car_ref)
        pad = jnp.zeros((page - t, w), F32)
        process([jnp.concatenate([kn_ref[...], pad], axis=0)],
                [jnp.concatenate([vn_ref[...], pad], axis=0)], True)

    process([r[...] for r in ck_refs], [r[...] for r in cv_refs], False)

    @pl.when(j == pl.num_programs(1) - 1)
    def _():
        heads_per_tile = LANES // head_dim
        lane = lax.broadcasted_iota(jnp.int32, (t, LANES), 1)
        for p in range(w // LANES):
            cols = slice(p * LANES, (p + 1) * LANES)
            r0 = p * heads_per_tile * t
            out = acc_ref[r0:r0 + t, cols]
            for h in range(1, heads_per_tile):
                out = jnp.where(lane >= h * head_dim, acc_ref[r0 + h * t:r0 + (h + 1) * t, cols], out)
            o_ref[:, cols] = out


def _attn_sample(q, k_new, v_new, cache_kt, cache_vt, page_table, bias_rows, layer, *, seq_len, head_dim):
    m, w = q.shape
    dbsz, n_pages = page_table.shape
    page = cache_kt.shape[3]
    heads = w // head_dim
    rows = heads * seq_len
    group = min(PAGE_GROUP, n_pages)
    assert page == LANES and seq_len == SUBLANES and n_pages % group == 0
    grid = (dbsz, n_pages // group)
    rowspec = pl.BlockSpec((seq_len, w), lambda b, j, pt: (b, 0))

    def pagespec(g):
        return pl.BlockSpec(
            (None, None, w, page),
            lambda b, j, pt: (layer, pt[b, n_pages - 1 - (j * group + g)], 0, 0))

    pagespecs = [pagespec(g) for g in range(group)]
    grid_spec = pltpu.PrefetchScalarGridSpec(
        num_scalar_prefetch=1, grid=grid,
        in_specs=[pl.BlockSpec((rows, LANES), lambda b, j, pt: (0, 0)),
                  rowspec, rowspec, rowspec] + pagespecs + pagespecs,
        out_specs=rowspec,
        scratch_shapes=[pltpu.VMEM((rows, w), BF16),
                        pltpu.VMEM((page, page), BF16),
                        pltpu.VMEM((rows, w), F32),
                        pltpu.VMEM((rows, LANES), F32)])
    return pl.pallas_call(
        functools.partial(_attn_sample_kernel, heads=heads, head_dim=head_dim, scale=head_dim ** -0.5,
                          group=group),
        grid_spec=grid_spec, out_shape=jax.ShapeDtypeStruct((m, w), F32),
        compiler_params=_params(2), name="attn_sample")(
            page_table, bias_rows, q, k_new, v_new, *([cache_kt] * group), *([cache_vt] * group))


def _out_proj_kernel(z_ref, o_ref, x_ref, p_ref, wout_ref, wpg_ref, wpe_ref, g_ref, b_ref,
                     x1b_ref, r_ref, *, alpha, row_chunk):
    chunks = [slice(r0, r0 + row_chunk) for r0 in range(0, x_ref.shape[0], row_chunk)]
    g, b = g_ref[...], b_ref[...]
    mixed = [jnp.concatenate([z_ref[rows, :].astype(BF16), o_ref[rows, :].astype(BF16)], axis=1)
             for rows in chunks]
    ys = [alpha * x_ref[rows, :] + _dot(mx, wout_ref[...]) for rows, mx in zip(chunks, mixed)]
    x1s = [_layer_norm(y, g, b) for y in ys]
    x1bs = [x1.astype(BF16) for x1 in x1s]
    logits = [_dot(x1b, wpg_ref[...]) for x1b in x1bs]
    pes = [_dot(p_ref[rows, :].astype(BF16), wpe_ref[...]) for rows in chunks]
    for rows, x1, x1b, lg, pe in zip(chunks, x1s, x1bs, logits, pes):
        x1b_ref[rows, :] = x1b
        r_ref[rows, :] = alpha * x1 + pe * (1.0 / (1.0 + jnp.exp(-lg)))


def _out_proj(z, o, x, p, w_out_b, w_pg_b, w_pe_b, ln_g, ln_b, layer, *, alpha):
    m, d = x.shape
    c = z.shape[1]
    pdim = p.shape[-1]
    tm = min(OUT_ROWS, m)
    rowspec = lambda width: pl.BlockSpec((tm, width), lambda i: (i, 0))
    const = lambda r, cc: pl.BlockSpec((None, r, cc), lambda i: (layer, 0, 0),
                                       pipeline_mode=pl.Buffered(1))
    return pl.pallas_call(
        functools.partial(_out_proj_kernel, alpha=alpha, row_chunk=min(OUT_ROW_CHUNK, tm)),
        grid=(m // tm,),
        in_specs=[rowspec(c), rowspec(c), rowspec(d),
                  pl.BlockSpec((None, tm, pdim), lambda i: (layer, i, 0)),
                  const(2 * c, d), const(d, d), const(pdim, d), const(1, d), const(1, d)],
        out_specs=(rowspec(d), rowspec(d)),
        out_shape=(jax.ShapeDtypeStruct((m, d), BF16), jax.ShapeDtypeStruct((m, d), F32)),
        compiler_params=_params(1), name="out_proj")(
            z, o, x, p, w_out_b, w_pg_b, w_pe_b, ln_g, ln_b)


def _ffn_kernel(*refs, seq_len, carry_mode, row_chunk):
    if carry_mode:
        (x1b_ref, r_ref, wa_ref, wg_ref, wconv_ref, bconv_ref, wd_ref, g_ref, b_ref,
         out_ref, tail_ref, carry_ref) = refs
    else:
        (x1b_ref, r_ref, wa_ref, wg_ref, wconv_ref, bconv_ref, wd_ref, g_ref, b_ref, e_ref,
         out_ref, tail_ref) = refs
    i = pl.program_id(0)
    f = pl.program_id(1)
    tm = x1b_ref.shape[0]

    @pl.when(f == 0)
    def _():
        out_ref[...] = r_ref[...]

    if carry_mode:
        @pl.when(i % (seq_len // tm) == 0)
        def _():
            carry_ref[f] = jnp.zeros(carry_ref.shape[1:], F32)

        prev8 = carry_ref[f]
    chunks = [slice(r0, r0 + row_chunk) for r0 in range(0, tm, row_chunk)]
    ups = [(_dot(x1b_ref[rows, :], wa_ref[...]), _dot(x1b_ref[rows, :], wg_ref[...])) for rows in chunks]
    hs = []
    for rows, (a, gate) in zip(chunks, ups):
        if carry_mode:
            a_c = _conv3_carry(a, prev8, wconv_ref[...])
            prev8 = a[row_chunk - SUBLANES:, :]
        else:
            a_c = _conv3_state(a, e_ref[rows, :], wconv_ref[...], seq_len)
            tail_ref[rows, :] = a
        hs.append((_gelu_tanh(a_c + bconv_ref[...]) * gate).astype(BF16))
    for rows, h in zip(chunks, hs):
        out_ref[rows, :] += _dot(h, wd_ref[...])
    if carry_mode:
        carry_ref[f] = prev8
        tail_ref[...] = prev8

    @pl.when(f == pl.num_programs(1) - 1)
    def _():
        out_ref[...] = _layer_norm(out_ref[...], g_ref[...], b_ref[...])


def _ffn(x1b, r, w_up_b, w_conv, b_conv, w_down_b, ln_g, ln_b, layer, *, seq_len, state=None):
    m, d = r.shape
    dff = w_conv.shape[-1]
    carry_mode = state is None
    tf = FFN_COLS
    nf = dff // tf
    tm = min(FFN_ROWS, m)
    assert dff % tf == 0
    rowspec = pl.BlockSpec((tm, d), lambda i, f: (i, 0))
    vec = lambda rows, width: pl.BlockSpec((None, rows, width), lambda i, f: (layer, 0, 0))
    in_specs = [rowspec, rowspec,
                pl.BlockSpec((None, d, tf), lambda i, f: (layer, 0, f)),
                pl.BlockSpec((None, d, tf), lambda i, f: (layer, 0, nf + f)),
                pl.BlockSpec((None, w_conv.shape[1], tf), lambda i, f: (layer, 0, f)),
                pl.BlockSpec((None, 1, tf), lambda i, f: (layer, 0, f)),
                pl.BlockSpec((None, tf, d), lambda i, f: (layer, f, 0)),
                vec(1, d), vec(1, d)]
    args = [x1b, r, w_up_b, w_up_b, w_conv, b_conv, w_down_b, ln_g, ln_b]
    scratch = []
    if carry_mode:
        assert seq_len % tm == 0
        tail_shape = jax.ShapeDtypeStruct((m // tm, SUBLANES, dff), F32)
        tail_spec = pl.BlockSpec((None, SUBLANES, tf), lambda i, f: (i, 0, f))
        scratch = [pltpu.VMEM((nf, SUBLANES, tf), F32)]
    else:
        assert tm == m and seq_len == SUBLANES
        tile = pl.BlockSpec((tm, tf), lambda i, f: (i, f))
        in_specs.append(tile)
        args.append(state)
        tail_shape = jax.ShapeDtypeStruct((m, dff), F32)
        tail_spec = tile
    return pl.pallas_call(
        functools.partial(_ffn_kernel, seq_len=seq_len, carry_mode=carry_mode,
                          row_chunk=min(FFN_ROW_CHUNK, tm)),
        grid=(m // tm, nf), in_specs=in_specs, out_specs=(rowspec, tail_spec),
        out_shape=(jax.ShapeDtypeStruct((m, d), F32), tail_shape),
        scratch_shapes=scratch, compiler_params=_params(2), name="ffn")(*args)


def _state_rows(state, seq_len):
    b, k1, c = state.shape
    return jnp.pad(state, ((0, 0), (0, seq_len - k1), (0, 0))).reshape(b * seq_len, c)


def _last_rows(tails, batch, n):
    per_seq = tails.shape[0] // batch
    return tails[per_seq - 1::per_seq, SUBLANES - n:]


def kernel(x_prompt, x_sample, cache_k, cache_v, state_conv_mix, state_conv_ffn, page_table, p_prompt,
           p_sample, w_in, w_conv_mix, sb_bias, w_out, ln1_g, ln1_b, w_up, w_ffn_conv, b_ffn_conv,
           w_down, w_pg, w_pe, ln2_g, ln2_b):
    bsz, seq, d = x_prompt.shape
    dbsz, dseq, _ = x_sample.shape
    depth, n_pool, page, heads, head_dim = cache_k.shape
    width = heads * head_dim
    conv_k = w_conv_mix.shape[1]
    assert conv_k == 3 and w_ffn_conv.shape[1] == 3
    alpha = (2.0 * depth) ** 0.25

    w_in_b, w_out_b, w_up_b = w_in.astype(BF16), w_out.astype(BF16), w_up.astype(BF16)
    w_down_b, w_pg_b, w_pe_b = w_down.astype(BF16), w_pg.astype(BF16), w_pe.astype(BF16)
    vec3 = lambda a: a.reshape(depth, 1, a.shape[-1])
    ln1_g3, ln1_b3, ln2_g3, ln2_b3, b_conv3 = map(vec3, (ln1_g, ln1_b, ln2_g, ln2_b, b_ffn_conv))
    cache_kt = cache_k.transpose(0, 1, 3, 4, 2).reshape(depth, n_pool, width, page)
    cache_vt = cache_v.transpose(0, 1, 3, 4, 2).reshape(depth, n_pool, width, page)
    pp = p_prompt.reshape(depth, bsz * seq, -1)
    ps = p_sample.reshape(depth, dbsz * dseq, -1)
    bias_rows = jnp.broadcast_to((sb_bias * LOG2E)[:, :, None, None], (depth, heads, dseq, LANES)
                                 ).reshape(depth, heads * dseq, LANES)

    xp = x_prompt.reshape(bsz * seq, d)
    xs = x_sample.reshape(dbsz * dseq, d)
    outs = [[] for _ in range(6)]
    kv = None
    for l in range(depth):
        shared_o = (w_out_b, w_pg_b, w_pe_b, ln1_g3, ln1_b3, l)
        shared_f = (w_up_b, w_ffn_conv, b_conv3, w_down_b, ln2_g3, ln2_b3, l)
        z, q, *kv, vb, mix_tail = _in_proj(xp, w_in_b, w_conv_mix, l, seq_len=seq, q_dtype=BF16,
                                           kv_buffers=kv)
        o = _attn_prompt(q, kv[0], vb, sb_bias, l, head_dim=head_dim)
        x1b, r = _out_proj(z, o, xp, pp, *shared_o, alpha=alpha)
        xp, ffn_tail = _ffn(x1b, r, *shared_f, seq_len=seq)
        outs[0].append(_last_rows(mix_tail, bsz, conv_k - 1))
        outs[1].append(_last_rows(ffn_tail, bsz, conv_k - 1))
        z, q, k, v, mix_tail = _in_proj(xs, w_in_b, w_conv_mix, l, seq_len=dseq, q_dtype=F32,
                                        state=_state_rows(state_conv_mix[l], dseq))
        o = _attn_sample(q, k, v, cache_kt, cache_vt, page_table, bias_rows[l], l,
                         seq_len=dseq, head_dim=head_dim)
        x1b, r = _out_proj(z, o, xs, ps, *shared_o, alpha=alpha)
        xs, ffn_tail = _ffn(x1b, r, *shared_f, seq_len=dseq, state=_state_rows(state_conv_ffn[l], dseq))
        outs[2].append(k.reshape(dbsz, dseq, heads, head_dim))
        outs[3].append(v.reshape(dbsz, dseq, heads, head_dim))
        outs[4].append(mix_tail.reshape(dbsz, dseq, -1)[:, dseq - (conv_k - 1):])
        outs[5].append(ffn_tail.reshape(dbsz, dseq, -1)[:, dseq - (conv_k - 1):])
    kv_prompt = [a.reshape(depth, bsz, heads, head_dim, seq).transpose(0, 1, 4, 2, 3) for a in kv]
    stacked = [jnp.stack(o) for o in outs]
    return (xp.reshape(bsz, seq, d), xs.reshape(dbsz, dseq, d), *kv_prompt, *stacked)
```

```python
import functools

import jax
import jax.numpy as jnp
from jax import lax
from jax.experimental import pallas as pl
from jax.experimental.pallas import tpu as pltpu

BF16 = jnp.bfloat16
F32 = jnp.float32

LN_EPS = 1e-5
LANES = 128
SUBLANES = 8
VMEM_LIMIT = 56 * 1024 * 1024

IN_ROWS = 512
IN_COLS = 512
ATTN_BLOCK = 256
ATTN_WIDTH = 8 * LANES
PAGE_GROUP = 16
OUT_ROWS = 512
OUT_ROW_CHUNK = 256
FFN_COLS = 256
FFN_ROWS = 1024
FFN_ROW_CHUNK = 512

_dot = functools.partial(jnp.dot, preferred_element_type=F32)


def _dot_nt(a, b):
    return lax.dot_general(a, b, (((1,), (1,)), ((), ())), preferred_element_type=F32)


def _params(n_grid):
    return pltpu.CompilerParams(dimension_semantics=("arbitrary",) * n_grid,
                                vmem_limit_bytes=VMEM_LIMIT)


def _shift_rows_carry(u, prev8, shift):
    p = pltpu.roll(u, shift, axis=0)
    row8 = lax.broadcasted_iota(jnp.int32, prev8.shape, 0)
    top = jnp.where(row8 < shift, pltpu.roll(prev8, shift, axis=0), p[0:SUBLANES])
    return jnp.concatenate([top, p[SUBLANES:]], axis=0)


def _conv3_carry(u, prev8, w):
    return (w[0:1] * _shift_rows_carry(u, prev8, 2) + w[1:2] * _shift_rows_carry(u, prev8, 1)
            + w[2:3] * u)


def _conv3_state(u, e, w, seq_len):
    tm = u.shape[0]
    pos = lax.broadcasted_iota(jnp.int32, u.shape, 0) % seq_len
    p1 = jnp.where(pos >= 1, pltpu.roll(u, 1, axis=0), pltpu.roll(e, tm - 1, axis=0))
    p2 = jnp.where(pos >= 2, pltpu.roll(u, 2, axis=0), e)
    return w[0:1] * p2 + w[1:2] * p1 + w[2:3] * u


def _layer_norm(y, g, b):
    mu = jnp.mean(y, axis=-1, keepdims=True)
    d = y - mu
    var = jnp.mean(d * d, axis=-1, keepdims=True)
    return d * lax.rsqrt(var + LN_EPS) * g + b


def _gelu_tanh(x):
    return x * (0.5 * (1.0 + jnp.tanh(0.7978845608028654 * (x + 0.044715 * (x * x * x)))))


LOG2E = 1.4426950408889634


def _log2_1m_beta(z2):
    nz = -z2
    return jnp.minimum(nz, 0.0) - jnp.log(1.0 + jnp.exp2(jnp.minimum(z2, nz))) * LOG2E


def _suffix_sums(lsn, tri):
    return _dot(lsn.astype(BF16), tri)


def _tri(n):
    r = lax.broadcasted_iota(jnp.int32, (n, n), 0)
    c = lax.broadcasted_iota(jnp.int32, (n, n), 1)
    return jnp.where(r >= c, 1.0, 0.0).astype(BF16)


def _in_proj_kernel(*refs, seq_len, carry_mode, n_aliased, layer):
    if carry_mode:
        refs = refs[:8] + refs[8 + n_aliased:]
        (x_ref, wb_ref, wc_ref, wx_ref, wq_ref, wk_ref, wv_ref, wconv_ref,
         z_ref, q_ref, kt_ref, vt_ref, vb_ref, tail_ref, carry_ref) = refs
    else:
        (x_ref, wb_ref, wc_ref, wx_ref, wq_ref, wk_ref, wv_ref, wconv_ref, e_ref,
         z_ref, q_ref, k_ref, v_ref, tail_ref) = refs
    tm = x_ref.shape[0]
    xb = x_ref[...].astype(BF16)
    u = _dot(xb, wc_ref[...]) * _dot(xb, wx_ref[...])
    if carry_mode:
        i = pl.program_id(1)

        @pl.when(i % (seq_len // tm) == 0)
        def _():
            carry_ref[...] = jnp.zeros_like(carry_ref)

        y = _conv3_carry(u, carry_ref[...], wconv_ref[...])
        carry_ref[...] = u[tm - SUBLANES:, :]
        tail_ref[...] = u[tm - SUBLANES:, :]
    else:
        y = _conv3_state(u, e_ref[...], wconv_ref[...], seq_len)
        tail_ref[...] = u
    z_ref[...] = (_dot(xb, wb_ref[...]) * y).astype(z_ref.dtype)
    q_ref[...] = _dot(xb, wq_ref[...]).astype(q_ref.dtype)
    k = _dot(xb, wk_ref[...])
    v = _dot(xb, wv_ref[...])
    if carry_mode:
        if n_aliased:
            kt_ref[...] = k.T
            vt_ref[...] = v.T
        else:
            for l in range(kt_ref.shape[0]):
                kt_ref[l] = k.T if l == layer else jnp.zeros(kt_ref.shape[1:], F32)
                vt_ref[l] = v.T if l == layer else jnp.zeros(vt_ref.shape[1:], F32)
        vb_ref[...] = v.astype(BF16)
    else:
        k_ref[...] = k
        v_ref[...] = v


def _in_proj(x, w_in_b, w_conv, layer, *, seq_len, q_dtype, state=None, kv_buffers=None):
    m, d = x.shape
    c = w_conv.shape[-1]
    carry_mode = state is None
    tm = min(IN_ROWS, m)
    tn = IN_COLS
    nb = c // tn
    batch = m // seq_len
    grid = (nb, m // tm)

    def wspec(s):
        return pl.BlockSpec((None, d, tn), lambda j, i, s=s: (layer, 0, s * nb + j))

    in_specs = [pl.BlockSpec((tm, d), lambda j, i: (i, 0))]
    in_specs += [wspec(s) for s in range(6)]
    in_specs += [pl.BlockSpec((None, w_conv.shape[1], tn), lambda j, i: (layer, 0, j))]
    args = [x] + [w_in_b] * 6 + [w_conv]
    tile = pl.BlockSpec((tm, tn), lambda j, i: (i, j))
    rows = lambda dt: jax.ShapeDtypeStruct((m, c), dt)
    scratch = []
    aliases = {}
    if carry_mode:
        assert seq_len % tm == 0
        per_seq = seq_len // tm
        depth = w_in_b.shape[0]
        tshape = jax.ShapeDtypeStruct((depth, batch, c, seq_len), F32)
        if kv_buffers is not None:
            tspec = pl.BlockSpec((None, None, tn, tm), lambda j, i: (layer, i // per_seq, j, i % per_seq))
            aliases = {len(args): 2, len(args) + 1: 3}
            in_specs += [pl.BlockSpec(memory_space=pl.ANY)] * 2
            args += list(kv_buffers)
        else:
            tspec = pl.BlockSpec((depth, None, tn, tm), lambda j, i: (0, i // per_seq, j, i % per_seq))
        tail_shape = jax.ShapeDtypeStruct((m // tm, SUBLANES, c), F32)
        tail_spec = pl.BlockSpec((None, SUBLANES, tn), lambda j, i: (i, 0, j))
        scratch = [pltpu.VMEM((SUBLANES, tn), F32)]
        out_shape = (rows(BF16), rows(q_dtype), tshape, tshape, rows(BF16), tail_shape)
        out_specs = (tile, tile, tspec, tspec, tile, tail_spec)
    else:
        assert tm == m and seq_len == SUBLANES
        in_specs.append(tile)
        args.append(state)
        out_shape = (rows(BF16), rows(q_dtype), rows(F32), rows(F32), rows(F32))
        out_specs = (tile,) * 5
    return pl.pallas_call(
        functools.partial(_in_proj_kernel, seq_len=seq_len, carry_mode=carry_mode, n_aliased=len(aliases),
                          layer=layer),
        grid=grid, in_specs=in_specs, out_specs=out_specs, input_output_aliases=aliases,
        out_shape=out_shape, scratch_shapes=scratch, compiler_params=_params(2),
        name="in_proj")(*args)


def _attn_prompt_kernel(bias_ref, q_ref, kt_ref, v_ref, o_ref,
                        kb_ref, tri_ref, acc_ref, car_ref, s_ref, a_ref, *, layer, tk, head_dim, scale):
    c = pl.program_id(1)
    qi = pl.program_id(2)
    tq, width = q_ref.shape
    heads_per_tile = LANES // head_dim
    n_heads = width // head_dim
    nk = kt_ref.shape[1] // tk
    hs = range(n_heads)
    tile = lambda h: slice((h // heads_per_tile) * LANES, (h // heads_per_tile + 1) * LANES)

    @pl.when(qi == 0)
    def _():
        chan = lax.broadcasted_iota(jnp.int32, (LANES, tk), 0)
        for j in range(nk):
            for h in hs:
                lo = (h % heads_per_tile) * head_dim
                kt = kt_ref[tile(h), j * tk:(j + 1) * tk] * (scale * LOG2E)
                kb_ref[j, h] = jnp.where((chan >= lo) & (chan < lo + head_dim), kt, 0.0).astype(BF16)
        tri_ref[...] = _tri(tk)

    acc_ref[...] = jnp.zeros_like(acc_ref)
    qs = [q_ref[:, tile(h)] for h in hs]
    bias2 = [bias_ref[layer, c * n_heads + h] * LOG2E for h in hs]

    def logit(j, h):
        return _dot(qs[h], kb_ref[j, h]) + bias2[h]

    tri = tri_ref[...]
    jn = jnp.maximum(qi - 1, 0)
    row = lax.broadcasted_iota(jnp.int32, (tq, tk), 0)
    col = lax.broadcasted_iota(jnp.int32, (tq, tk), 1)
    valid = col < row
    z_diag, z_next, locs = [logit(qi, 0)], [], []
    for h in hs:
        if h + 1 < n_heads:
            z_diag.append(logit(qi, h + 1))
        lsn = jnp.where(valid, _log2_1m_beta(z_diag[h]), 0.0)
        z_next.append(logit(jn, h))
        locs.append(_suffix_sums(lsn, tri))
    for h in hs:
        a = jnp.where(valid, jnp.exp2(z_diag[h] + locs[h]), 0.0)
        car_ref[h] = jnp.broadcast_to(locs[h][:, 0:1], (tq, LANES))
        a_ref[h] = a.astype(BF16)
        s_ref[h] = z_next[h]

    def body(it, j_prev):
        j = qi - 1 - it
        jn = jnp.maximum(j - 1, 0)
        koff = pl.multiple_of(j_prev * tk, tk)
        tri = tri_ref[...]
        n = len(hs)
        locs = [None] * n

        def apply_prev(h):
            acc_ref[h] += _dot(a_ref[h], v_ref[pl.ds(koff, tk), tile(h)])

        def finish(h):
            carry = car_ref[h]
            a = jnp.exp2(s_ref[h] + locs[h] + jnp.tile(carry, (1, tk // LANES)))
            car_ref[h] = carry + jnp.broadcast_to(locs[h][:, 0:1], carry.shape)
            a_ref[h] = a.astype(BF16)

        apply_prev(0)
        for h in hs:
            lsn = _log2_1m_beta(s_ref[h])
            if h + 1 < n:
                apply_prev(h + 1)
            locs[h] = _suffix_sums(lsn, tri)
        for h in hs:
            z_next = logit(jn, h)
            finish(h)
            s_ref[h] = z_next
        return j

    j_last = lax.fori_loop(0, qi, body, qi)
    koff = pl.multiple_of(j_last * tk, tk)
    outs = [_dot(a_ref[h], v_ref[pl.ds(koff, tk), tile(h)]) for h in hs]

    lane = lax.broadcasted_iota(jnp.int32, (tq, LANES), 1)
    totals = [acc_ref[h] + outs[h] for h in hs]
    for t in range(width // LANES):
        out = totals[t * heads_per_tile]
        for h in range(1, heads_per_tile):
            out = jnp.where(lane >= h * head_dim, totals[t * heads_per_tile + h], out)
        o_ref[:, t * LANES:(t + 1) * LANES] = out.astype(o_ref.dtype)


def _attn_prompt(q, kt, v, sb_bias, layer, *, head_dim):
    m, w = q.shape
    _, batch, _, seq_len = kt.shape
    tq = tk = ATTN_BLOCK
    width = min(ATTN_WIDTH, w)
    n_heads = width // head_dim
    nq = seq_len // tq
    grid = (batch, w // width, nq)
    qspec = pl.BlockSpec((tq, width), lambda b, c, i: (b * nq + i, c))
    return pl.pallas_call(
        functools.partial(_attn_prompt_kernel, layer=layer, tk=tk, head_dim=head_dim,
                          scale=head_dim ** -0.5),
        grid=grid,
        in_specs=[pl.BlockSpec(memory_space=pltpu.SMEM), qspec,
                  pl.BlockSpec((None, None, width, seq_len), lambda b, c, i: (layer, b, c, 0)),
                  pl.BlockSpec((seq_len, width), lambda b, c, i: (b, c))],
        out_specs=qspec,
        out_shape=jax.ShapeDtypeStruct((m, w), BF16),
        scratch_shapes=[pltpu.VMEM((seq_len // tk, n_heads, LANES, tk), BF16),
                        pltpu.VMEM((tk, tk), BF16),
                        pltpu.VMEM((n_heads, tq, LANES), F32),
                        pltpu.VMEM((n_heads, tq, LANES), F32),
                        pltpu.VMEM((n_heads, tq, tk), F32),
                        pltpu.VMEM((n_heads, tq, tk), BF16)],
        compiler_params=_params(3), name="attn_prompt")(sb_bias, q, kt, v)


def _attn_sample_kernel(pt_ref, bias_ref, q_ref, kn_ref, vn_ref, *refs, heads, head_dim, scale, group):
    del pt_ref
    ck_refs, cv_refs = refs[:group], refs[group:2 * group]
    o_ref, wt_ref, tri_ref, acc_ref, car_ref = refs[2 * group:]
    j = pl.program_id(1)
    t, w = q_ref.shape
    page = ck_refs[0].shape[1]
    rows = heads * t

    def process(ks, vs, new_rows):
        wt, tri, bias2 = wt_ref[...], tri_ref[...], bias_ref[...]
        qk = _dot_nt if new_rows else _dot
        zs = [qk(wt, k.astype(BF16)) * (scale * LOG2E) + bias2 for k in ks]
        lsns = [_log2_1m_beta(z) for z in zs]
        if new_rows:
            row = lax.broadcasted_iota(jnp.int32, (rows, page), 0)
            col = lax.broadcasted_iota(jnp.int32, (rows, page), 1)
            valid = col < row % t
            lsns = [jnp.where(valid, lsn, 0.0) for lsn in lsns]
        locs = [_suffix_sums(lsn, tri) for lsn in lsns]
        carry = car_ref[...]
        avs = []
        for z, loc in zip(zs, locs):
            avs.append(jnp.exp2(z + loc + carry))
            carry = carry + jnp.broadcast_to(loc[:, 0:1], carry.shape)
        car_ref[...] = carry
        if new_rows:
            avs = [jnp.where(valid, a, 0.0) for a in avs]
        a_all = jnp.concatenate([a.astype(BF16) for a in avs], axis=1)
        if new_rows:
            acc_ref[...] += _dot(a_all, jnp.concatenate([v.astype(BF16) for v in vs], axis=0))
        else:
            acc_ref[...] += _dot_nt(a_all, jnp.concatenate([v.astype(BF16) for v in vs], axis=1))

    @pl.when(j == 0)
    def _():
        qt = jnp.tile(q_ref[...], (heads, 1))
        rowh = lax.broadcasted_iota(jnp.int32, (rows, w), 0) // t
        colh = lax.broadcasted_iota(jnp.int32, (rows, w), 1) // head_dim
        wt_ref[...] = jnp.where(rowh == colh, qt, 0.0).astype(BF16)
        tri_ref[...] = _tri(page)
        acc_ref[...] = jnp.zeros_like(acc_ref)
        car_ref[...] = jnp.zeros_like(car_ref)
        pad = jnp.zeros((page - t, w), F32)
        process([jnp.concatenate([kn_ref[...], pad], axis=0)],
                [jnp.concatenate([vn_ref[...], pad], axis=0)], True)

    process([r[...] for r in ck_refs], [r[...] for r in cv_refs], False)

    @pl.when(j == pl.num_programs(1) - 1)
    def _():
        heads_per_tile = LANES // head_dim
        lane = lax.broadcasted_iota(jnp.int32, (t, LANES), 1)
        for p in range(w // LANES):
            cols = slice(p * LANES, (p + 1) * LANES)
            r0 = p * heads_per_tile * t
            out = acc_ref[r0:r0 + t, cols]
            for h in range(1, heads_per_tile):
                out = jnp.where(lane >= h * head_dim, acc_ref[r0 + h * t:r0 + (h + 1) * t, cols], out)
            o_ref[:, cols] = out


def _attn_sample(q, k_new, v_new, cache_kt, cache_vt, page_table, bias_rows, layer, *, seq_len, head_dim):
    m, w = q.shape
    dbsz, n_pages = page_table.shape
    page = cache_kt.shape[3]
    heads = w // head_dim
    rows = heads * seq_len
    group = min(PAGE_GROUP, n_pages)
    assert page == LANES and seq_len == SUBLANES and n_pages % group == 0
    grid = (dbsz, n_pages // group)
    rowspec = pl.BlockSpec((seq_len, w), lambda b, j, pt: (b, 0))

    def pagespec(g):
        return pl.BlockSpec(
            (None, None, w, page),
            lambda b, j, pt: (layer, pt[b, n_pages - 1 - (j * group + g)], 0, 0))

    pagespecs = [pagespec(g) for g in range(group)]
    grid_spec = pltpu.PrefetchScalarGridSpec(
        num_scalar_prefetch=1, grid=grid,
        in_specs=[pl.BlockSpec((rows, LANES), lambda b, j, pt: (0, 0)),
                  rowspec, rowspec, rowspec] + pagespecs + pagespecs,
        out_specs=rowspec,
        scratch_shapes=[pltpu.VMEM((rows, w), BF16),
                        pltpu.VMEM((page, page), BF16),
                        pltpu.VMEM((rows, w), F32),
                        pltpu.VMEM((rows, LANES), F32)])
    return pl.pallas_call(
        functools.partial(_attn_sample_kernel, heads=heads, head_dim=head_dim, scale=head_dim ** -0.5,
                          group=group),
        grid_spec=grid_spec, out_shape=jax.ShapeDtypeStruct((m, w), F32),
        compiler_params=_params(2), name="attn_sample")(
            page_table, bias_rows, q, k_new, v_new, *([cache_kt] * group), *([cache_vt] * group))


def _out_proj_kernel(z_ref, o_ref, x_ref, p_ref, wout_ref, wpg_ref, wpe_ref, g_ref, b_ref,
                     x1b_ref, r_ref, *, alpha, row_chunk):
    chunks = [slice(r0, r0 + row_chunk) for r0 in range(0, x_ref.shape[0], row_chunk)]
    g, b = g_ref[...], b_ref[...]
    mixed = [jnp.concatenate([z_ref[rows, :].astype(BF16), o_ref[rows, :].astype(BF16)], axis=1)
             for rows in chunks]
    ys = [alpha * x_ref[rows, :] + _dot(mx, wout_ref[...]) for rows, mx in zip(chunks, mixed)]
    x1s = [_layer_norm(y, g, b) for y in ys]
    x1bs = [x1.astype(BF16) for x1 in x1s]
    logits = [_dot(x1b, wpg_ref[...]) for x1b in x1bs]
    pes = [_dot(p_ref[rows, :].astype(BF16), wpe_ref[...]) for rows in chunks]
    for rows, x1, x1b, lg, pe in zip(chunks, x1s, x1bs, logits, pes):
        x1b_ref[rows, :] = x1b
        r_ref[rows, :] = alpha * x1 + pe * (1.0 / (1.0 + jnp.exp(-lg)))


def _out_proj(z, o, x, p, w_out_b, w_pg_b, w_pe_b, ln_g, ln_b, layer, *, alpha):
    m, d = x.shape
    c = z.shape[1]
    pdim = p.shape[-1]
    tm = min(OUT_ROWS, m)
    rowspec = lambda width: pl.BlockSpec((tm, width), lambda i: (i, 0))
    const = lambda r, cc: pl.BlockSpec((None, r, cc), lambda i: (layer, 0, 0),
                                       pipeline_mode=pl.Buffered(1))
    return pl.pallas_call(
        functools.partial(_out_proj_kernel, alpha=alpha, row_chunk=min(OUT_ROW_CHUNK, tm)),
        grid=(m // tm,),
        in_specs=[rowspec(c), rowspec(c), rowspec(d),
                  pl.BlockSpec((None, tm, pdim), lambda i: (layer, i, 0)),
                  const(2 * c, d), const(d, d), const(pdim, d), const(1, d), const(1, d)],
        out_specs=(rowspec(d), rowspec(d)),
        out_shape=(jax.ShapeDtypeStruct((m, d), BF16), jax.ShapeDtypeStruct((m, d), F32)),
        compiler_params=_params(1), name="out_proj")(
            z, o, x, p, w_out_b, w_pg_b, w_pe_b, ln_g, ln_b)


def _ffn_kernel(*refs, seq_len, carry_mode, row_chunk):
    if carry_mode:
        (x1b_ref, r_ref, wa_ref, wg_ref, wconv_ref, bconv_ref, wd_ref, g_ref, b_ref,
         out_ref, tail_ref, carry_ref) = refs
    else:
        (x1b_ref, r_ref, wa_ref, wg_ref, wconv_ref, bconv_ref, wd_ref, g_ref, b_ref, e_ref,
         out_ref, tail_ref) = refs
    i = pl.program_id(0)
    f = pl.program_id(1)
    tm = x1b_ref.shape[0]

    @pl.when(f == 0)
    def _():
        out_ref[...] = r_ref[...]

    if carry_mode:
        @pl.when(i % (seq_len // tm) == 0)
        def _():
            carry_ref[f] = jnp.zeros(carry_ref.shape[1:], F32)

        prev8 = carry_ref[f]
    chunks = [slice(r0, r0 + row_chunk) for r0 in range(0, tm, row_chunk)]
    ups = [(_dot(x1b_ref[rows, :], wa_ref[...]), _dot(x1b_ref[rows, :], wg_ref[...])) for rows in chunks]
    hs = []
    for rows, (a, gate) in zip(chunks, ups):
        if carry_mode:
            a_c = _conv3_carry(a, prev8, wconv_ref[...])
            prev8 = a[row_chunk - SUBLANES:, :]
        else:
            a_c = _conv3_state(a, e_ref[rows, :], wconv_ref[...], seq_len)
            tail_ref[rows, :] = a
        hs.append((_gelu_tanh(a_c + bconv_ref[...]) * gate).astype(BF16))
    for rows, h in zip(chunks, hs):
        out_ref[rows, :] += _dot(h, wd_ref[...])
    if carry_mode:
        carry_ref[f] = prev8
        tail_ref[...] = prev8

    @pl.when(f == pl.num_programs(1) - 1)
    def _():
        out_ref[...] = _layer_norm(out_ref[...], g_ref[...], b_ref[...])


def _ffn(x1b, r, w_up_b, w_conv, b_conv, w_down_b, ln_g, ln_b, layer, *, seq_len, state=None):
    m, d = r.shape
    dff = w_conv.shape[-1]
    carry_mode = state is None
    tf = FFN_COLS
    nf = dff // tf
    tm = min(FFN_ROWS, m)
    assert dff % tf == 0
    rowspec = pl.BlockSpec((tm, d), lambda i, f: (i, 0))
    vec = lambda rows, width: pl.BlockSpec((None, rows, width), lambda i, f: (layer, 0, 0))
    in_specs = [rowspec, rowspec,
                pl.BlockSpec((None, d, tf), lambda i, f: (layer, 0, f)),
                pl.BlockSpec((None, d, tf), lambda i, f: (layer, 0, nf + f)),
                pl.BlockSpec((None, w_conv.shape[1], tf), lambda i, f: (layer, 0, f)),
                pl.BlockSpec((None, 1, tf), lambda i, f: (layer, 0, f)),
                pl.BlockSpec((None, tf, d), lambda i, f: (layer, f, 0)),
                vec(1, d), vec(1, d)]
    args = [x1b, r, w_up_b, w_up_b, w_conv, b_conv, w_down_b, ln_g, ln_b]
    scratch = []
    if carry_mode:
        assert seq_len % tm == 0
        tail_shape = jax.ShapeDtypeStruct((m // tm, SUBLANES, dff), F32)
        tail_spec = pl.BlockSpec((None, SUBLANES, tf), lambda i, f: (i, 0, f))
        scratch = [pltpu.VMEM((nf, SUBLANES, tf), F32)]
    else:
        assert tm == m and seq_len == SUBLANES
        tile = pl.BlockSpec((tm, tf), lambda i, f: (i, f))
        in_specs.append(tile)
        args.append(state)
        tail_shape = jax.ShapeDtypeStruct((m, dff), F32)
        tail_spec = tile
    return pl.pallas_call(
        functools.partial(_ffn_kernel, seq_len=seq_len, carry_mode=carry_mode,
                          row_chunk=min(FFN_ROW_CHUNK, tm)),
        grid=(m // tm, nf), in_specs=in_specs, out_specs=(rowspec, tail_spec),
        out_shape=(jax.ShapeDtypeStruct((m, d), F32), tail_shape),
        scratch_shapes=scratch, compiler_params=_params(2), name="ffn")(*args)


def _state_rows(state, seq_len):
    b, k1, c = state.shape
    return jnp.pad(state, ((0, 0), (0, seq_len - k1), (0, 0))).reshape(b * seq_len, c)


def _last_rows(tails, batch, n):
    per_seq = tails.shape[0] // batch
    return tails[per_seq - 1::per_seq, SUBLANES - n:]


def kernel(x_prompt, x_sample, cache_k, cache_v, state_conv_mix, state_conv_ffn, page_table, p_prompt,
           p_sample, w_in, w_conv_mix, sb_bias, w_out, ln1_g, ln1_b, w_up, w_ffn_conv, b_ffn_conv,
           w_down, w_pg, w_pe, ln2_g, ln2_b):
    bsz, seq, d = x_prompt.shape
    dbsz, dseq, _ = x_sample.shape
    depth, n_pool, page, heads, head_dim = cache_k.shape
    width = heads * head_dim
    conv_k = w_conv_mix.shape[1]
    assert conv_k == 3 and w_ffn_conv.shape[1] == 3
    alpha = (2.0 * depth) ** 0.25

    w_in_b, w_out_b, w_up_b = w_in.astype(BF16), w_out.astype(BF16), w_up.astype(BF16)
    w_down_b, w_pg_b, w_pe_b = w_down.astype(BF16), w_pg.astype(BF16), w_pe.astype(BF16)
    vec3 = lambda a: a.reshape(depth, 1, a.shape[-1])
    ln1_g3, ln1_b3, ln2_g3, ln2_b3, b_conv3 = map(vec3, (ln1_g, ln1_b, ln2_g, ln2_b, b_ffn_conv))
    cache_kt = cache_k.transpose(0, 1, 3, 4, 2).reshape(depth, n_pool, width, page)
    cache_vt = cache_v.transpose(0, 1, 3, 4, 2).reshape(depth, n_pool, width, page)
    pp = p_prompt.reshape(depth, bsz * seq, -1)
    ps = p_sample.reshape(depth, dbsz * dseq, -1)
    bias_rows = jnp.broadcast_to((sb_bias * LOG2E)[:, :, None, None], (depth, heads, dseq, LANES)
                                 ).reshape(depth, heads * dseq, LANES)

    xp = x_prompt.reshape(bsz * seq, d)
    xs = x_sample.reshape(dbsz * dseq, d)
    outs = [[] for _ in range(6)]
    kv = None
    for l in range(depth):
        shared_o = (w_out_b, w_pg_b, w_pe_b, ln1_g3, ln1_b3, l)
        shared_f = (w_up_b, w_ffn_conv, b_conv3, w_down_b, ln2_g3, ln2_b3, l)
        z, q, *kv, vb, mix_tail = _in_proj(xp, w_in_b, w_conv_mix, l, seq_len=seq, q_dtype=BF16,
                                           kv_buffers=kv)
        o = _attn_prompt(q, kv[0], vb, sb_bias, l, head_dim=head_dim)
        x1b, r = _out_proj(z, o, xp, pp, *shared_o, alpha=alpha)
        xp, ffn_tail = _ffn(x1b, r, *shared_f, seq_len=seq)
        outs[0].append(_last_rows(mix_tail, bsz, conv_k - 1))
        outs[1].append(_last_rows(ffn_tail, bsz, conv_k - 1))
        z, q, k, v, mix_tail = _in_proj(xs, w_in_b, w_conv_mix, l, seq_len=dseq, q_dtype=F32,
                                        state=_state_rows(state_conv_mix[l], dseq))
        o = _attn_sample(q, k, v, cache_kt, cache_vt, page_table, bias_rows[l], l,
                         seq_len=dseq, head_dim=head_dim)
        x1b, r = _out_proj(z, o, xs, ps, *shared_o, alpha=alpha)
        xs, ffn_tail = _ffn(x1b, r, *shared_f, seq_len=dseq, state=_state_rows(state_conv_ffn[l], dseq))
        outs[2].append(k.reshape(dbsz, dseq, heads, head_dim))
        outs[3].append(v.reshape(dbsz, dseq, heads, head_dim))
        outs[4].append(mix_tail.reshape(dbsz, dseq, -1)[:, dseq - (conv_k - 1):])
        outs[5].append(ffn_tail.reshape(dbsz, dseq, -1)[:, dseq - (conv_k - 1):])
    kv_prompt = [a.reshape(depth, bsz, heads, head_dim, seq).transpose(0, 1, 4, 2, 3) for a in kv]
    stacked = [jnp.stack(o) for o in outs]
    return (xp.reshape(bsz, seq, d), xs.reshape(dbsz, dseq, d), *kv_prompt, *stacked)
```
